```python
import math
import jax, jax.numpy as jnp
from jax import lax
import numpy as np

D_MODEL = 1024
BATCH = 4
SEQ = 8192
DEPTH = 2

HEAD_DIM = 64
A_HEADS = 8
A_BRANCHES = ((128, 1), (512, 4), (2048, 16))
A_BLOCK = 64
B_HEADS = 8
B_KV_HEADS = 2
B_QBLOCK = 128
GRID_W = 64
ROPE_THETA = 10000.0
C_HEADS = 16
C_KV_HEADS = 4
C_RADIUS = 128
C_BLOCK = 128
NUM_BUCKETS = 32
MAX_EXACT = 8
MAX_DISTANCE = 1024
BIAS_HEADS = 16
N_EXPERTS = 16
D_EXPERT = 2048
CAPACITY_FACTOR = 2
EPS = 1e-6
NEG_INF = -1e30
A_W = A_HEADS * HEAD_DIM
B_QW = B_HEADS * HEAD_DIM
B_KVW = B_KV_HEADS * HEAD_DIM
C_QW = C_HEADS * HEAD_DIM
C_KVW = C_KV_HEADS * HEAD_DIM
L0_IN = 3 * A_W + B_QW + 2 * B_KVW
L0_OUT = A_W + B_QW
L1_IN = C_QW + 2 * C_KVW

kernel_name = 'hybrid_dilated_axial_window_ec_encoder'


def rms_norm(x, g):
    xf = x.astype(jnp.float32)
    y = xf * lax.rsqrt(jnp.mean(xf * xf, axis=-1, keepdims=True) + EPS)
    return (y * g.astype(jnp.float32)).astype(x.dtype)


def t5_bucket(rel):
    half = NUM_BUCKETS // 2
    ret = jnp.where(rel > 0, half, 0)
    n = jnp.abs(rel)
    nf = jnp.maximum(n, 1).astype(jnp.float32)
    large = MAX_EXACT + (jnp.log(nf / MAX_EXACT) / math.log(MAX_DISTANCE / MAX_EXACT)
                         * (half - MAX_EXACT)).astype(jnp.int32)
    large = jnp.minimum(large, half - 1)
    return ret + jnp.where(n < MAX_EXACT, n, large)


def local_offsets(block):
    return jnp.arange(3 * block)[None, :] - block - jnp.arange(block)[:, None]


def split_heads(t, n):
    b, s, _ = t.shape
    return t.reshape(b, s, n, HEAD_DIM).transpose(0, 2, 1, 3)


def band_attention(q, k, v, radius, block, bias, sink):
    b, hq, length, hd = q.shape
    hkv = k.shape[1]
    g = hq // hkv
    nb = -(-length // block)
    lp = nb * block
    qp = jnp.pad(q, ((0, 0), (0, 0), (0, lp - length), (0, 0))).reshape(b, hkv, g, nb, block, hd)
    kv_pad = ((0, 0), (0, 0), (block, lp - length + block), (0, 0))

    def windows(t):
        tb = jnp.pad(t, kv_pad).reshape(b, hkv, nb + 2, block, hd)
        return jnp.concatenate([tb[:, :, :-2], tb[:, :, 1:-1], tb[:, :, 2:]], axis=3)

    kw = windows(k)
    vw = windows(v)
    s = jnp.einsum('bhgnqd,bhnkd->bhgnqk', qp, kw).astype(jnp.float32) / math.sqrt(hd)
    s = s + bias.astype(jnp.float32).reshape(hkv, g, 1, block, 3 * block)
    rel = local_offsets(block)
    kpos = jnp.arange(nb)[:, None] * block - block + jnp.arange(3 * block)[None, :]
    valid = (jnp.abs(rel) <= radius)[None] & ((kpos >= 0) & (kpos < length))[:, None, :]
    s = jnp.where(valid, s, NEG_INF)
    m = jnp.max(s, axis=-1)
    if sink is not None:
        sk = sink.astype(jnp.float32).reshape(hkv, g, 1, 1)
        m = jnp.maximum(m, sk)
    p = jnp.exp(s - m[..., None])
    l = jnp.sum(p, axis=-1)
    if sink is not None:
        l = l + jnp.exp(sk - m)
    o = jnp.einsum('bhgnqk,bhnkd->bhgnqd', (p / l[..., None]).astype(v.dtype), vw)
    o = o.reshape(b, hq, lp, hd)[:, :, :length]
    return o, m.reshape(b, hq, lp)[:, :, :length], l.reshape(b, hq, lp)[:, :, :length]


def dilated_attention(q, k, v, rel_bias):
    b, h, s, hd = q.shape
    rel = local_offsets(A_BLOCK)
    outs, ms, ls = [], [], []
    for window, dil in A_BRANCHES:
        radius = window // (2 * dil)
        ls_len = s // dil

        def to_sub(t):
            return t.reshape(b, h, ls_len, dil, hd).transpose(0, 3, 1, 2, 4).reshape(b * dil, h, ls_len, hd)

        bias = rel_bias[:A_HEADS][:, t5_bucket(rel * dil)]
        o, m, l = band_attention(to_sub(q), to_sub(k), to_sub(v), radius, A_BLOCK, bias, None)
        outs.append(o.reshape(b, dil, h, ls_len, hd).transpose(0, 2, 3, 1, 4).reshape(b, h, s, hd))
        ms.append(m.reshape(b, dil, h, ls_len).transpose(0, 2, 3, 1).reshape(b, h, s))
        ls.append(l.reshape(b, dil, h, ls_len).transpose(0, 2, 3, 1).reshape(b, h, s))
    m_all = jnp.stack(ms)
    w = jnp.exp(m_all - jnp.max(m_all, axis=0, keepdims=True)) * jnp.stack(ls)
    o_all = jnp.stack(outs).astype(jnp.float32)
    out = jnp.sum(w[..., None] * o_all, axis=0) / jnp.sum(w, axis=0)[..., None]
    return out.astype(q.dtype)


def axial_rope(x):
    s = x.shape[2]
    rows = s // GRID_W
    row = jnp.broadcast_to(jnp.arange(rows)[:, None], (rows, GRID_W)).reshape(s)
    col = jnp.broadcast_to(jnp.arange(GRID_W)[None, :], (rows, GRID_W)).reshape(s)
    half = HEAD_DIM // 2
    inv = ROPE_THETA ** (-jnp.arange(0, half, 2, dtype=jnp.float32) / half)

    def rot(xh, pos):
        ang = pos.astype(jnp.float32)[:, None] * inv
        cos, sin = jnp.cos(ang), jnp.sin(ang)
        x1, x2 = jnp.split(xh.astype(jnp.float32), 2, axis=-1)
        return jnp.concatenate([x1 * cos - x2 * sin, x1 * sin + x2 * cos], axis=-1)

    return jnp.concatenate([rot(x[..., :half], row), rot(x[..., half:], col)], axis=-1).astype(x.dtype)


def dense_block_attention(q, k, v):
    b, hq, s, hd = q.shape
    hkv = k.shape[1]
    g = hq // hkv
    nqb = s // B_QBLOCK
    qb = q.reshape(b, hkv, g, nqb, B_QBLOCK, hd).transpose(3, 0, 1, 2, 4, 5)

    def one_block(qblk):
        sc = jnp.einsum('bhgqd,bhkd->bhgqk', qblk, k).astype(jnp.float32) / math.sqrt(hd)
        p = jax.nn.softmax(sc, axis=-1)
        return jnp.einsum('bhgqk,bhkd->bhgqd', p.astype(v.dtype), v)

    o = lax.map(one_block, qb)
    return o.transpose(1, 2, 3, 0, 4, 5).reshape(b, hq, s, hd)


def mixer_ab(h, w_in, a_qn, a_kn, b_qn, b_kn, w_out, rel_bias):
    b, s, _ = h.shape
    proj = jnp.einsum('bsd,de->bse', h, w_in)
    cuts = [A_W, 2 * A_W, 3 * A_W, 3 * A_W + B_QW, 3 * A_W + B_QW + B_KVW]
    aq, ak, av, bq, bk, bv = jnp.split(proj, cuts, axis=-1)
    oa = dilated_attention(rms_norm(split_heads(aq, A_HEADS), a_qn),
                           rms_norm(split_heads(ak, A_HEADS), a_kn),
                           split_heads(av, A_HEADS), rel_bias)
    ob = dense_block_attention(axial_rope(rms_norm(split_heads(bq, B_HEADS), b_qn)),
                               axial_rope(rms_norm(split_heads(bk, B_KV_HEADS), b_kn)),
                               split_heads(bv, B_KV_HEADS))
    o = jnp.concatenate([oa, ob], axis=1).transpose(0, 2, 1, 3).reshape(b, s, L0_OUT)
    return jnp.einsum('bse,ed->bsd', o, w_out)


def mixer_c(h, w_in, c_qn, c_kn, sink, w_out, rel_bias):
    b, s, _ = h.shape
    proj = jnp.einsum('bsd,de->bse', h, w_in)
    cq, ck, cv = jnp.split(proj, [C_QW, C_QW + C_KVW], axis=-1)
    bias = rel_bias[:, t5_bucket(local_offsets(C_BLOCK))]
    o, _, _ = band_attention(rms_norm(split_heads(cq, C_HEADS), c_qn),
                             rms_norm(split_heads(ck, C_KV_HEADS), c_kn),
                             split_heads(cv, C_KV_HEADS), C_RADIUS, C_BLOCK, bias, sink)
    o = o.transpose(0, 2, 1, 3).reshape(b, s, C_QW)
    return jnp.einsum('bse,ed->bsd', o, w_out)


def ec_moe(h, w_router, w_gate, w_up, w_down):
    b, s, d = h.shape
    cap = CAPACITY_FACTOR * s // N_EXPERTS
    aff = jax.nn.softmax(jnp.einsum('bsd,de->bse', h, w_router).astype(jnp.float32), axis=-1)
    gates, idx = lax.top_k(jnp.swapaxes(aff, 1, 2), cap)
    xs = jax.vmap(lambda hb, ib: hb[ib])(h, idx)
    act = jax.nn.silu(jnp.einsum('becd,edf->becf', xs, w_gate)) * jnp.einsum('becd,edf->becf', xs, w_up)
    y = jnp.einsum('becf,efd->becd', act, w_down) * gates[..., None].astype(h.dtype)

    def combine(ib, yb):
        return jnp.zeros((s, d), yb.dtype).at[ib.reshape(-1)].add(yb.reshape(-1, d))

    return jax.vmap(combine)(idx, y)


def setup_inputs(seed: int = 0) -> dict:
    key = jax.random.key(seed)
    ks = jax.random.split(key, 25)

    def nrm(k, shape, scale):
        return jax.random.normal(k, shape, jnp.float32) * scale

    def gain(k, n):
        return 1.0 + 0.02 * jax.random.normal(k, (n,), jnp.float32)

    return {
        'x': nrm(ks[0], (BATCH, SEQ, D_MODEL), 1.0),
        'rel_bias': nrm(ks[1], (BIAS_HEADS, NUM_BUCKETS), 0.5),
        'l0_norm_attn': gain(ks[2], D_MODEL),
        'l0_w_in': nrm(ks[3], (D_MODEL, L0_IN), D_MODEL ** -0.5),
        'l0_a_qnorm': gain(ks[4], HEAD_DIM),
        'l0_a_knorm': gain(ks[5], HEAD_DIM),
        'l0_b_qnorm': gain(ks[6], HEAD_DIM),
        'l0_b_knorm': gain(ks[7], HEAD_DIM),
        'l0_w_out': nrm(ks[8], (L0_OUT, D_MODEL), L0_OUT ** -0.5),
        'l0_norm_ffn': gain(ks[9], D_MODEL),
        'l0_router': nrm(ks[10], (D_MODEL, N_EXPERTS), D_MODEL ** -0.5),
        'l0_w_gate': nrm(ks[11], (N_EXPERTS, D_MODEL, D_EXPERT), D_MODEL ** -0.5),
        'l0_w_up': nrm(ks[12], (N_EXPERTS, D_MODEL, D_EXPERT), D_MODEL ** -0.5),
        'l0_w_down': nrm(ks[13], (N_EXPERTS, D_EXPERT, D_MODEL), D_EXPERT ** -0.5),
        'l1_norm_attn': gain(ks[14], D_MODEL),
        'l1_w_in': nrm(ks[15], (D_MODEL, L1_IN), D_MODEL ** -0.5),
        'l1_c_qnorm': gain(ks[16], HEAD_DIM),
        'l1_c_knorm': gain(ks[17], HEAD_DIM),
        'l1_sink': nrm(ks[18], (C_HEADS,), 1.0),
        'l1_w_out': nrm(ks[19], (C_QW, D_MODEL), C_QW ** -0.5),
        'l1_norm_ffn': gain(ks[20], D_MODEL),
        'l1_router': nrm(ks[21], (D_MODEL, N_EXPERTS), D_MODEL ** -0.5),
        'l1_w_gate': nrm(ks[22], (N_EXPERTS, D_MODEL, D_EXPERT), D_MODEL ** -0.5),
        'l1_w_up': nrm(ks[23], (N_EXPERTS, D_MODEL, D_EXPERT), D_MODEL ** -0.5),
        'l1_w_down': nrm(ks[24], (N_EXPERTS, D_EXPERT, D_MODEL), D_EXPERT ** -0.5),
    }


def reference(x, rel_bias,
              l0_norm_attn, l0_w_in, l0_a_qnorm, l0_a_knorm, l0_b_qnorm, l0_b_knorm, l0_w_out,
              l0_norm_ffn, l0_router, l0_w_gate, l0_w_up, l0_w_down,
              l1_norm_attn, l1_w_in, l1_c_qnorm, l1_c_knorm, l1_sink, l1_w_out,
              l1_norm_ffn, l1_router, l1_w_gate, l1_w_up, l1_w_down):
    attn_norms = [l0_norm_attn, l1_norm_attn]
    mixers = [
        lambda h: mixer_ab(h, l0_w_in, l0_a_qnorm, l0_a_knorm, l0_b_qnorm, l0_b_knorm, l0_w_out, rel_bias),
        lambda h: mixer_c(h, l1_w_in, l1_c_qnorm, l1_c_knorm, l1_sink, l1_w_out, rel_bias),
    ]
    ffns = [
        (l0_norm_ffn, l0_router, l0_w_gate, l0_w_up, l0_w_down),
        (l1_norm_ffn, l1_router, l1_w_gate, l1_w_up, l1_w_down),
    ]
    for i in range(DEPTH):
        x = x + mixers[i](rms_norm(x, attn_norms[i]))
        g, wr, wg, wu, wd = ffns[i]
        x = x + ec_moe(rms_norm(x, g), wr, wg, wu, wd)
    return x
```

```python
import functools
import math

import jax
import jax.numpy as jnp
from jax import lax
from jax.experimental import pallas as pl
from jax.experimental.pallas import tpu as pltpu

F32 = jnp.float32
BF16 = jnp.bfloat16
I32 = jnp.int32

LANES = 128
HEAD_DIM = 64
NEG_INF = -1e30
EPS = 1e-6

A_HEADS = 8
A_BRANCHES = ((128, 1), (512, 4), (2048, 16))
B_HEADS = 8
B_KV_HEADS = 2
GRID_W = 64
ROPE_THETA = 10000.0
C_HEADS = 16
C_KV_HEADS = 4
C_RADIUS = 128
NUM_BUCKETS = 32
MAX_EXACT = 8
MAX_DISTANCE = 1024
N_EXPERTS = 16
CAPACITY_FACTOR = 2

MIB = 1024 * 1024
NT_DIMS = (((1,), (1,)), ((), ()))


def _params(semantics, vmem_mib):
    return pltpu.CompilerParams(dimension_semantics=semantics, vmem_limit_bytes=vmem_mib * MIB)


def _load_rows(x_ref, tm, nt, tiled):
    if not tiled:
        return x_ref[...]
    return jnp.concatenate([x_ref[pl.ds(s, tm, stride=nt), :] for s in range(nt)], axis=1)


def _store_rows_tiled(o_ref, val, tm, nt):
    for s in range(nt):
        o_ref[pl.ds(s, tm, stride=nt), :] = val[:, s * LANES:(s + 1) * LANES]


def _proj_body(x_ref, g_ref, w_ref, bd_ref, gains_ref, cos_ref, sin_ref, *out_refs, groups, tm, nt, tiled):
    x = _load_rows(x_ref, tm, nt, tiled)
    ms = jnp.mean(x * x, axis=-1, keepdims=True)
    xn = (x * lax.rsqrt(ms + EPS) * g_ref[...]).astype(BF16)
    lane = lax.broadcasted_iota(I32, (1, LANES), 1)
    first16 = (lane % 32) < 16
    bd = bd_ref[...]
    for (c0, width, kind, grow), o_ref in zip(groups, out_refs):
        y = jnp.dot(xn, w_ref[:, c0:c0 + width], preferred_element_type=F32)
        if kind == "plain":
            o_ref[...] = y.astype(o_ref.dtype)
            continue
        gain = gains_ref[grow:grow + 1, :]
        for j in range(width // LANES):
            yj = y[:, j * LANES:(j + 1) * LANES]
            sq = yj * yj
            hi = sq.astype(BF16)
            lo = (sq - hi.astype(F32)).astype(BF16)
            ss = jnp.dot(hi, bd, preferred_element_type=F32) + jnp.dot(lo, bd, preferred_element_type=F32)
            yn = yj * lax.rsqrt(ss * (1.0 / HEAD_DIM) + EPS) * gain
            if kind == "rope":
                partner = jnp.where(first16, pltpu.roll(yn, LANES - 16, 1), pltpu.roll(yn, 16, 1))
                yn = yn * cos_ref[...] + partner * sin_ref[...]
            o_ref[:, j * LANES:(j + 1) * LANES] = yn.astype(o_ref.dtype)


def _norm_proj(x, g, w, bd, gains, cos_t, sin_t, groups, *, seq, tiled, tm=512):
    d_model = w.shape[0]
    nt = d_model // LANES
    tokens = x.shape[0] // nt if tiled else x.shape[0]
    tm = min(tm, seq)
    nblk_seq = seq // tm
    x_spec = (pl.BlockSpec((tm * nt, LANES), lambda i: (i, 0)) if tiled
              else pl.BlockSpec((tm, d_model), lambda i: (i, 0)))
    const = lambda i: (0, 0)
    in_specs = [
        x_spec,
        pl.BlockSpec(g.shape, const),
        pl.BlockSpec(w.shape, const),
        pl.BlockSpec(bd.shape, const),
        pl.BlockSpec(gains.shape, const),
        pl.BlockSpec((tm, LANES), lambda i: (i % nblk_seq, 0)),
        pl.BlockSpec((tm, LANES), lambda i: (i % nblk_seq, 0)),
    ]
    out_shape = [jax.ShapeDtypeStruct((tokens, wd), BF16) for (_, wd, _, _) in groups]
    out_specs = [pl.BlockSpec((tm, wd), lambda i: (i, 0)) for (_, wd, _, _) in groups]
    return pl.pallas_call(
        functools.partial(_proj_body, groups=groups, tm=tm, nt=nt, tiled=tiled),
        grid=(tokens // tm,),
        in_specs=in_specs,
        out_specs=out_specs,
        out_shape=out_shape,
        compiler_params=_params(("parallel",), 48),
        name="norm_proj",
    )(x, g, w, bd, gains, cos_t, sin_t)


def _band_body(*refs, tq, qr, halo, length, has_state, final, has_sink):
    it = iter(refs)
    sink_ref = next(it) if has_sink else None
    q_ref, kp_ref, km_ref, kn_ref, vp_ref, vm_ref, vn_ref, bias_ref = (next(it) for _ in range(8))
    if has_state:
        m_in, l_in, a_in = next(it), next(it), next(it)
    if final:
        o_ref = next(it)
    else:
        m_out, l_out, a_out = next(it), next(it), next(it)

    cb = pl.program_id(1)
    blk = pl.program_id(2)
    win = qr + 2 * halo
    kext = jnp.concatenate([kp_ref[0], km_ref[0], kn_ref[0]], axis=0)
    vext = jnp.concatenate([vp_ref[0], vm_ref[0], vn_ref[0]], axis=0)
    lane = lax.broadcasted_iota(I32, (1, LANES), 1)
    lo = lane < HEAD_DIM
    col = lax.broadcasted_iota(I32, (1, win), 1)
    head_row = lax.broadcasted_iota(I32, (2 * qr, 1), 0) < qr
    bias = bias_ref[0]
    zero = jnp.zeros((), BF16)

    def split(tile):
        return jnp.concatenate([tile[:, 0:1], tile[:, HEAD_DIM:HEAD_DIM + 1]], axis=0)

    def merge(colv):
        return jnp.where(lo, colv[:qr], colv[qr:])

    for s in range(tq // qr):
        rows = slice(s * qr, (s + 1) * qr)
        qs = q_ref[0, rows, :]
        q2 = jnp.concatenate([jnp.where(lo, qs, zero), jnp.where(lo, zero, qs)], axis=0)
        kw = kext[s * qr:s * qr + win]
        vw = vext[s * qr:s * qr + win]
        sc = lax.dot_general(q2, kw, NT_DIMS, preferred_element_type=F32) + bias
        kpos = blk * tq + (s * qr - halo) + col
        sc = jnp.where((kpos >= 0) & (kpos < length), sc, NEG_INF)
        m_cur = jnp.max(sc, axis=1, keepdims=True)
        if has_state:
            m_prev = split(m_in[0, rows, :])
            l_prev = split(l_in[0, rows, :])
            m_new = jnp.maximum(m_prev, m_cur)
        else:
            m_new = m_cur
        if has_sink:
            sk = jnp.where(head_row, sink_ref[2 * cb], sink_ref[2 * cb + 1])
            m_new = jnp.maximum(m_new, sk)
        p = jnp.exp(sc - m_new)
        l_new = jnp.sum(p, axis=1, keepdims=True)
        pv = jnp.dot(p.astype(BF16), vw, preferred_element_type=F32)
        acc = merge(pv)
        if has_state:
            alpha = jnp.exp(m_prev - m_new)
            l_new = l_new + alpha * l_prev
            acc = acc + merge(alpha) * a_in[0, rows, :]
        if has_sink:
            l_new = l_new + jnp.exp(sk - m_new)
        if final:
            o_ref[0, rows, :] = (acc / merge(l_new)).astype(o_ref.dtype)
        else:
            m_out[0, rows, :] = jnp.broadcast_to(merge(m_new), (qr, LANES))
            l_out[0, rows, :] = jnp.broadcast_to(merge(l_new), (qr, LANES))
            a_out[0, rows, :] = acc


def _band_attention(q, k, v, bias, *, kv_div, bias_mod, qr, halo, state=None, final=True, sink=None, tq=512):
    bsz, length, qcols = q.shape
    ncb = qcols // LANES
    tq = min(tq, length)
    nq = length // tq
    per = tq // halo
    nh = length // halo
    win = qr + 2 * halo

    def qmap(b, cb, i, *_):
        return (b, i, cb)

    def kmain(b, cb, i, *_):
        return (b, i, cb // kv_div)

    def kprev(b, cb, i, *_):
        return (b, jnp.maximum(i * per - 1, 0), cb // kv_div)

    def knext(b, cb, i, *_):
        return (b, jnp.minimum((i + 1) * per, nh - 1), cb // kv_div)

    def bmap(b, cb, i, *_):
        return (cb % bias_mod, 0, 0)

    main = lambda m: pl.BlockSpec((1, tq, LANES), m)
    edge = lambda m: pl.BlockSpec((1, halo, LANES), m)
    in_specs = [main(qmap), edge(kprev), main(kmain), edge(knext), edge(kprev), main(kmain), edge(knext),
                pl.BlockSpec((1, 2 * qr, win), bmap)]
    args = [q, k, k, k, v, v, v, bias]
    if state is not None:
        in_specs += [main(qmap)] * 3
        args += list(state)
    if final:
        out_shape = jax.ShapeDtypeStruct(q.shape, BF16)
        out_specs = main(qmap)
    else:
        out_shape = [jax.ShapeDtypeStruct(q.shape, F32)] * 3
        out_specs = [main(qmap)] * 3
    has_sink = sink is not None
    grid_spec = pltpu.PrefetchScalarGridSpec(
        num_scalar_prefetch=1 if has_sink else 0,
        grid=(bsz, ncb, nq),
        in_specs=in_specs,
        out_specs=out_specs,
    )
    body = functools.partial(_band_body, tq=tq, qr=qr, halo=halo, length=length,
                             has_state=state is not None, final=final, has_sink=has_sink)
    call = pl.pallas_call(body, grid_spec=grid_spec, out_shape=out_shape,
                          compiler_params=_params(("parallel", "parallel", "parallel"), 48),
                          name="band_attention")
    return call(sink, *args) if has_sink else call(*args)


def _dense_body(q_ref, k_ref, v_ref, o_ref, q4_ref, m_ref, l_ref, acc_ref, *, tq, tk, nk):
    lane = lax.broadcasted_iota(I32, (1, LANES), 1)
    lo = lane < HEAD_DIM
    zero = jnp.zeros((), BF16)
    for h in range(4):
        blk = q_ref[0, :, (h // 2) * LANES:(h // 2 + 1) * LANES]
        q4_ref[h * tq:(h + 1) * tq, :] = jnp.where(lo, blk, zero) if h % 2 == 0 else jnp.where(lo, zero, blk)
    m_ref[...] = jnp.full(m_ref.shape, NEG_INF, F32)
    l_ref[...] = jnp.zeros(l_ref.shape, F32)
    acc_ref[...] = jnp.zeros(acc_ref.shape, F32)

    def step(j, carry):
        start = pl.multiple_of(j * tk, tk)
        ks = k_ref[0, pl.ds(start, tk), :]
        vs = v_ref[0, pl.ds(start, tk), :]
        sc = lax.dot_general(q4_ref[...], ks, NT_DIMS, preferred_element_type=F32)
        m_prev = m_ref[...]
        m_new = jnp.maximum(m_prev, jnp.max(sc, axis=1, keepdims=True))
        alpha = jnp.exp(m_prev - m_new)
        p = jnp.exp(sc - m_new)
        l_ref[...] = alpha * l_ref[...] + jnp.sum(p, axis=1, keepdims=True)
        acc_ref[...] = alpha * acc_ref[...] + jnp.dot(p.astype(BF16), vs, preferred_element_type=F32)
        m_ref[...] = m_new
        return carry

    lax.fori_loop(0, nk, step, 0)
    o = acc_ref[...] / l_ref[...]
    for hp in range(2):
        o_ref[0, :, hp * LANES:(hp + 1) * LANES] = jnp.where(
            lo, o[(2 * hp) * tq:(2 * hp + 1) * tq], o[(2 * hp + 1) * tq:(2 * hp + 2) * tq]).astype(o_ref.dtype)


def _dense_attention(q, k, v, *, tq=256, tk=512):
    bsz, seq, _ = q.shape
    tq = min(tq, seq)
    tk = min(tk, seq)
    ngroups = k.shape[2] // LANES
    return pl.pallas_call(
        functools.partial(_dense_body, tq=tq, tk=tk, nk=seq // tk),
        grid=(bsz, ngroups, seq // tq),
        in_specs=[
            pl.BlockSpec((1, tq, 2 * LANES), lambda b, g, i: (b, i, g)),
            pl.BlockSpec((1, seq, LANES), lambda b, g, i: (b, 0, g)),
            pl.BlockSpec((1, seq, LANES), lambda b, g, i: (b, 0, g)),
        ],
        out_specs=pl.BlockSpec((1, tq, 2 * LANES), lambda b, g, i: (b, i, g)),
        out_shape=jax.ShapeDtypeStruct(q.shape, BF16),
        scratch_shapes=[
            pltpu.VMEM((4 * tq, LANES), BF16),
            pltpu.VMEM((4 * tq, 1), F32),
            pltpu.VMEM((4 * tq, 1), F32),
            pltpu.VMEM((4 * tq, LANES), F32),
        ],
        compiler_params=_params(("parallel", "parallel", "parallel"), 48),
        name="dense_attention",
    )(q, k, v)


def _outproj_body(*refs, n_in, tm, nt, tiled):
    o_refs = refs[:n_in]
    w_ref, x_ref, g_ref, wr_ref, x2_ref, hn_ref, aff_ref = refs[n_in:]
    y = None
    c0 = 0
    for o_ref in o_refs:
        wd = o_ref.shape[1]
        part = jnp.dot(o_ref[...], w_ref[c0:c0 + wd, :], preferred_element_type=F32)
        y = part if y is None else y + part
        c0 += wd
    x2 = _load_rows(x_ref, tm, nt, tiled) + y
    ms = jnp.mean(x2 * x2, axis=-1, keepdims=True)
    hn = x2 * lax.rsqrt(ms + EPS) * g_ref[...]
    _store_rows_tiled(x2_ref, x2, tm, nt)
    _store_rows_tiled(hn_ref, hn, tm, nt)
    wr = wr_ref[...]
    wr_hi = wr.astype(BF16)
    wr_lo = (wr - wr_hi.astype(F32)).astype(BF16)
    hn_hi = hn.astype(BF16)
    hn_lo = (hn - hn_hi.astype(F32)).astype(BF16)
    lt = (lax.dot_general(wr_hi, hn_hi, NT_DIMS, preferred_element_type=F32)
          + lax.dot_general(wr_lo, hn_hi, NT_DIMS, preferred_element_type=F32)
          + lax.dot_general(wr_hi, hn_lo, NT_DIMS, preferred_element_type=F32))
    ex = jnp.exp(lt - jnp.max(lt, axis=0, keepdims=True))
    aff_ref[0] = ex / jnp.sum(ex, axis=0, keepdims=True)


def _out_proj(o_parts, w, x, g, wr_t, *, bsz, seq, tiled, tm=512):
    d_model = w.shape[1]
    nt = d_model // LANES
    tokens = bsz * seq
    tm = min(tm, seq)
    nblk_seq = seq // tm
    nexp = wr_t.shape[0]
    const = lambda i: (0, 0)
    x_spec = (pl.BlockSpec((tm * nt, LANES), lambda i: (i, 0)) if tiled
              else pl.BlockSpec((tm, d_model), lambda i: (i, 0)))
    in_specs = [pl.BlockSpec((tm, o.shape[1]), lambda i: (i, 0)) for o in o_parts]
    in_specs += [pl.BlockSpec(w.shape, const), x_spec, pl.BlockSpec(g.shape, const), pl.BlockSpec(wr_t.shape, const)]
    tiled_spec = pl.BlockSpec((tm * nt, LANES), lambda i: (i, 0))
    return pl.pallas_call(
        functools.partial(_outproj_body, n_in=len(o_parts), tm=tm, nt=nt, tiled=tiled),
        grid=(tokens // tm,),
        in_specs=in_specs,
        out_specs=[tiled_spec, tiled_spec,
                   pl.BlockSpec((1, nexp, tm), lambda i: (i // nblk_seq, 0, i % nblk_seq))],
        out_shape=[jax.ShapeDtypeStruct((tokens * nt, LANES), F32),
                   jax.ShapeDtypeStruct((tokens * nt, LANES), F32),
                   jax.ShapeDtypeStruct((bsz, nexp, seq), F32)],
        compiler_params=_params(("parallel",), 48),
        name="out_proj_router",
    )(*o_parts, w, x, g, wr_t)


CUMSUM_BLOCK = 256
INDEX_CHUNK = 128


def _lane_cumsum(x, tri):
    outs = []
    carry = jnp.zeros((x.shape[0], 1), F32)
    for k in range(x.shape[1] // CUMSUM_BLOCK):
        blk = x[:, k * CUMSUM_BLOCK:(k + 1) * CUMSUM_BLOCK].astype(BF16)
        c = jnp.dot(blk, tri, preferred_element_type=F32) + carry
        outs.append(c)
        carry = c[:, CUMSUM_BLOCK - 1:CUMSUM_BLOCK]
    return jnp.concatenate(outs, axis=1)


def _route_body(aff_ref, idx_ref, cs_ref, *, seq, cap, nexp):
    e = pl.program_id(1)

    @pl.when(e == 0)
    def _():
        aff = aff_ref[0]
        bits = pltpu.bitcast(aff, I32)

        def search(_, c):
            lo, hi = c
            mid = lo + ((hi - lo + 1) >> 1)
            cnt = jnp.sum(jnp.where(bits >= mid, 1.0, 0.0), axis=1, keepdims=True)
            ok = cnt >= cap
            return jnp.where(ok, mid, lo), jnp.where(ok, hi, mid - 1)

        lo0 = jnp.zeros((nexp, 1), I32)
        hi0 = jnp.full((nexp, 1), 0x7F800000, I32)
        thr, _ = lax.fori_loop(0, 32, search, (lo0, hi0))
        gt = bits > thr
        eq = bits == thr
        r = lax.broadcasted_iota(I32, (CUMSUM_BLOCK, CUMSUM_BLOCK), 0)
        c = lax.broadcasted_iota(I32, (CUMSUM_BLOCK, CUMSUM_BLOCK), 1)
        tri = jnp.where(r <= c, 1.0, 0.0).astype(BF16)
        need = cap - jnp.sum(jnp.where(gt, 1.0, 0.0), axis=1, keepdims=True)
        eq_rank = _lane_cumsum(jnp.where(eq, 1.0, 0.0), tri)
        sel = gt | (eq & (eq_rank <= need))
        cs = _lane_cumsum(jnp.where(sel, 1.0, 0.0), tri)
        for ee in range(nexp):
            cs_ref[ee] = cs[ee:ee + 1, :]

    ones = jnp.ones((8, LANES), BF16)
    for cc in range(cap // INDEX_CHUNK):
        slot = (lax.broadcasted_iota(I32, (INDEX_CHUNK, LANES), 0) + cc * INDEX_CHUNK).astype(F32)

        def tile(k, part):
            row = cs_ref[e, :, pl.ds(pl.multiple_of(k * LANES, LANES), LANES)]
            return part + jnp.where(row <= slot, 1.0, 0.0)

        part = lax.fori_loop(0, seq // LANES, tile, jnp.zeros((INDEX_CHUNK, LANES), F32))
        cnt = lax.dot_general(ones, part.astype(BF16), NT_DIMS, preferred_element_type=F32)
        idx_ref[0, :, cc * INDEX_CHUNK:(cc + 1) * INDEX_CHUNK] = cnt[0:1].astype(I32)


def _route(aff_t, cap):
    bsz, nexp, seq = aff_t.shape
    return pl.pallas_call(
        functools.partial(_route_body, seq=seq, cap=cap, nexp=nexp),
        grid=(bsz, nexp),
        in_specs=[pl.BlockSpec((1, nexp, seq), lambda b, e: (b, 0, 0))],
        out_specs=pl.BlockSpec((1, 1, cap), lambda b, e: (b * nexp + e, 0, 0)),
        out_shape=jax.ShapeDtypeStruct((bsz * nexp, 1, cap), I32),
        scratch_shapes=[pltpu.VMEM((nexp, 1, seq), F32)],
        compiler_params=_params(("parallel", "arbitrary"), 48),
        name="route",
    )(aff_t)


ROW_UNROLL = 8


def _gather_body(idx_ref, h_hbm, x_ref, hbuf, xbuf, sem, *, cap, nt):
    b = pl.program_id(0)
    e = pl.program_id(1)

    @pl.when(e == 0)
    def _():
        cp = pltpu.make_async_copy(h_hbm.at[b], hbuf, sem)
        cp.start()
        cp.wait()

    def rows(i, carry):
        for u in range(ROW_UNROLL):
            c = i * ROW_UNROLL + u
            t = idx_ref[0, 0, c]
            xbuf[pl.ds(pl.multiple_of(c * nt, nt), nt), :] = hbuf[pl.ds(pl.multiple_of(t * nt, nt), nt), :]
        return carry

    lax.fori_loop(0, cap // ROW_UNROLL, rows, 0)
    for s in range(nt):
        x_ref[0, :, s * LANES:(s + 1) * LANES] = xbuf[pl.ds(s, cap, stride=nt), :].astype(x_ref.dtype)


def _gather(idx, h_tiled, *, bsz, nexp, seq, cap, nt):
    return pl.pallas_call(
        functools.partial(_gather_body, cap=cap, nt=nt),
        grid=(bsz, nexp),
        in_specs=[pl.BlockSpec((1, 1, cap), lambda b, e: (b * nexp + e, 0, 0), memory_space=pltpu.SMEM),
                  pl.BlockSpec(memory_space=pl.ANY)],
        out_specs=pl.BlockSpec((1, cap, nt * LANES), lambda b, e: (e, b, 0)),
        out_shape=jax.ShapeDtypeStruct((nexp, bsz * cap, nt * LANES), BF16),
        scratch_shapes=[pltpu.VMEM((seq * nt, LANES), F32), pltpu.VMEM((cap * nt, LANES), F32),
                        pltpu.SemaphoreType.DMA(())],
        compiler_params=_params(("arbitrary", "arbitrary"), 56),
        name="moe_gather",
    )(idx, h_tiled)


def _ffn_body(x_ref, wg_ref, wu_ref, wd_ref, y_ref, acc_ref, *, cap, nt):
    f = pl.program_id(2)
    x = x_ref[0]
    g = jnp.dot(x, wg_ref[0].astype(BF16), preferred_element_type=F32)
    u = jnp.dot(x, wu_ref[0].astype(BF16), preferred_element_type=F32)
    act = (g * (1.0 / (1.0 + jnp.exp(-g))) * u).astype(BF16)
    y = jnp.dot(act, wd_ref[0].astype(BF16), preferred_element_type=F32)

    @pl.when(f == 0)
    def _():
        acc_ref[...] = y

    @pl.when(f > 0)
    def _():
        acc_ref[...] += y

    @pl.when(f == pl.num_programs(2) - 1)
    def _():
        _store_rows_tiled(y_ref.at[0], acc_ref[...], cap, nt)


def _ffn(x, wg, wu, wd, *, bsz, cap, tf=512):
    nexp, d_model, d_exp = wg.shape
    nt = d_model // LANES
    tf = min(tf, d_exp)
    return pl.pallas_call(
        functools.partial(_ffn_body, cap=cap, nt=nt),
        grid=(nexp, bsz, d_exp // tf),
        in_specs=[pl.BlockSpec((1, cap, d_model), lambda e, b, f: (e, b, 0)),
                  pl.BlockSpec((1, d_model, tf), lambda e, b, f: (e, 0, f)),
                  pl.BlockSpec((1, d_model, tf), lambda e, b, f: (e, 0, f)),
                  pl.BlockSpec((1, tf, d_model), lambda e, b, f: (e, f, 0))],
        out_specs=pl.BlockSpec((1, cap * nt, LANES), lambda e, b, f: (b * nexp + e, 0, 0)),
        out_shape=jax.ShapeDtypeStruct((bsz * nexp, cap * nt, LANES), F32),
        scratch_shapes=[pltpu.VMEM((cap, d_model), F32)],
        compiler_params=_params(("parallel", "parallel", "arbitrary"), 56),
        name="moe_ffn",
    )(x, wg, wu, wd)


def _combine_body(idx_ref, aff_ref, x_hbm, y_ref, o_hbm, acc, sem, *, cap, nt, nexp):
    b = pl.program_id(0)
    e = pl.program_id(1)

    @pl.when(e == 0)
    def _():
        cp = pltpu.make_async_copy(x_hbm.at[b], acc, sem)
        cp.start()
        cp.wait()

    def rows(i, carry):
        dst, vals = [], []
        for u in range(ROW_UNROLL):
            c = i * ROW_UNROLL + u
            t = idx_ref[0, 0, c]
            gate = aff_ref[0, 0, t]
            d = pl.ds(pl.multiple_of(t * nt, nt), nt)
            dst.append(d)
            vals.append(acc[d, :] + gate * y_ref[0, pl.ds(pl.multiple_of(c * nt, nt), nt), :])
        for d, val in zip(dst, vals):
            acc[d, :] = val
        return carry

    lax.fori_loop(0, cap // ROW_UNROLL, rows, 0)

    @pl.when(e == nexp - 1)
    def _():
        cp = pltpu.make_async_copy(acc, o_hbm.at[b], sem)
        cp.start()
        cp.wait()


def _combine(idx, aff_rows, x_tiled, y_tiled, *, bsz, nexp, seq, cap, nt):
    return pl.pallas_call(
        functools.partial(_combine_body, cap=cap, nt=nt, nexp=nexp),
        grid=(bsz, nexp),
        in_specs=[pl.BlockSpec((1, 1, cap), lambda b, e: (b * nexp + e, 0, 0), memory_space=pltpu.SMEM),
                  pl.BlockSpec((1, 1, seq), lambda b, e: (b * nexp + e, 0, 0), memory_space=pltpu.SMEM),
                  pl.BlockSpec(memory_space=pl.ANY),
                  pl.BlockSpec((1, cap * nt, LANES), lambda b, e: (b * nexp + e, 0, 0))],
        out_specs=pl.BlockSpec(memory_space=pl.ANY),
        out_shape=jax.ShapeDtypeStruct(x_tiled.shape, F32),
        scratch_shapes=[pltpu.VMEM((seq * nt, LANES), F32), pltpu.SemaphoreType.DMA(())],
        compiler_params=_params(("arbitrary", "arbitrary"), 56),
        name="moe_combine",
    )(idx, aff_rows, x_tiled, y_tiled)


def _moe(x2_t, hn_t, aff_t, wg, wu, wd, *, bsz, seq):
    nexp = aff_t.shape[1]
    nt = wg.shape[1] // LANES
    cap = CAPACITY_FACTOR * seq // nexp
    idx = _route(aff_t, cap)
    xg = _gather(idx, hn_t.reshape(bsz, seq * nt, LANES), bsz=bsz, nexp=nexp, seq=seq, cap=cap, nt=nt)
    y = _ffn(xg, wg, wu, wd, bsz=bsz, cap=cap)
    out = _combine(idx, aff_t.reshape(bsz * nexp, 1, seq), x2_t.reshape(bsz, seq * nt, LANES), y,
                   bsz=bsz, nexp=nexp, seq=seq, cap=cap, nt=nt)
    return out.reshape(bsz * seq * nt, LANES)


def _t5_bucket(rel):
    half = NUM_BUCKETS // 2
    ret = jnp.where(rel > 0, half, 0)
    n = jnp.abs(rel)
    nf = jnp.maximum(n, 1).astype(F32)
    large = MAX_EXACT + (jnp.log(nf / MAX_EXACT) / math.log(MAX_DISTANCE / MAX_EXACT)
                         * (half - MAX_EXACT)).astype(I32)
    large = jnp.minimum(large, half - 1)
    return ret + jnp.where(n < MAX_EXACT, n, large)


def _band_bias(rel_bias, nheads, qr, halo, radius, dil):
    rel = jnp.arange(qr + 2 * halo)[None, :] - halo - jnp.arange(qr)[:, None]
    tile = rel_bias[:nheads][:, _t5_bucket(rel * dil)].astype(F32)
    tile = jnp.where((jnp.abs(rel) <= radius)[None], tile, NEG_INF)
    return tile.reshape(nheads // 2, 2 * qr, qr + 2 * halo)


def _tile_gain(g, scale=1.0):
    return jnp.tile(g.astype(F32) * scale, LANES // HEAD_DIM)[None, :]


def _dup_heads(w, nheads):
    d = w.shape[0]
    w4 = w.reshape(d, nheads, 1, HEAD_DIM)
    return jnp.broadcast_to(w4, (d, nheads, 2, HEAD_DIM)).reshape(d, nheads * 2 * HEAD_DIM)


def _rope_tables(seq):
    half = HEAD_DIM // 2
    inv = ROPE_THETA ** (-jnp.arange(0, half, 2, dtype=F32) / half)
    t = jnp.arange(seq)
    row = (t // GRID_W).astype(F32)[:, None] * inv
    colp = (t % GRID_W).astype(F32)[:, None] * inv
    cos64 = jnp.concatenate([jnp.cos(row), jnp.cos(row), jnp.cos(colp), jnp.cos(colp)], axis=1)
    sin64 = jnp.concatenate([-jnp.sin(row), jnp.sin(row), -jnp.sin(colp), jnp.sin(colp)], axis=1)
    return jnp.tile(cos64, (1, 2)), jnp.tile(sin64, (1, 2))


def _block_diag_ones():
    r = jnp.arange(LANES)
    return (r[:, None] // HEAD_DIM == r[None, :] // HEAD_DIM).astype(BF16)


def kernel(x, rel_bias, l0_norm_attn, l0_w_in, l0_a_qnorm, l0_a_knorm, l0_b_qnorm, l0_b_knorm, l0_w_out,
           l0_norm_ffn, l0_router, l0_w_gate, l0_w_up, l0_w_down, l1_norm_attn, l1_w_in, l1_c_qnorm, l1_c_knorm,
           l1_sink, l1_w_out, l1_norm_ffn, l1_router, l1_w_gate, l1_w_up, l1_w_down):
    bsz, seq, d_model = x.shape
    nt = d_model // LANES
    scale = 1.0 / math.sqrt(HEAD_DIM)
    a_w = A_HEADS * HEAD_DIM
    bq_w = B_HEADS * HEAD_DIM
    bkv_w = B_KV_HEADS * HEAD_DIM
    cq_w = C_HEADS * HEAD_DIM
    ckv_w = C_KV_HEADS * HEAD_DIM
    bd = _block_diag_ones()
    cos_t, sin_t = _rope_tables(seq)

    c = 3 * a_w + bq_w
    w0 = jnp.concatenate([l0_w_in[:, :c], _dup_heads(l0_w_in[:, c:c + bkv_w], B_KV_HEADS),
                          _dup_heads(l0_w_in[:, c + bkv_w:], B_KV_HEADS)], axis=1).astype(BF16)
    gains0 = jnp.concatenate([_tile_gain(l0_a_qnorm, scale), _tile_gain(l0_a_knorm),
                              _tile_gain(l0_b_qnorm, scale), _tile_gain(l0_b_knorm)], axis=0)
    groups0 = ((0, a_w, "norm", 0), (a_w, a_w, "norm", 1), (2 * a_w, a_w, "plain", 0),
               (3 * a_w, bq_w, "rope", 2), (c, 2 * bkv_w, "rope", 3), (c + 2 * bkv_w, 2 * bkv_w, "plain", 0))
    qa, ka, va, qb, kb, vb = _norm_proj(x.reshape(bsz * seq, d_model), l0_norm_attn[None, :], w0, bd, gains0,
                                        cos_t, sin_t, groups0, seq=seq, tiled=False)
    state = None
    for n, (window, dil) in enumerate(A_BRANCHES):
        radius = window // (2 * dil)
        view = lambda t: t.reshape(bsz, seq // dil, dil * t.shape[-1])
        bias = _band_bias(rel_bias, A_HEADS, 2 * radius, radius, radius, dil)
        last = n == len(A_BRANCHES) - 1
        res = _band_attention(view(qa), view(ka), view(va), bias, kv_div=1, bias_mod=A_HEADS // 2,
                              qr=2 * radius, halo=radius, state=None if state is None else tuple(view(t) for t in state),
                              final=last)
        if last:
            oa = res.reshape(bsz * seq, a_w)
        else:
            state = tuple(t.reshape(bsz * seq, a_w) for t in res)
    ob = _dense_attention(qb.reshape(bsz, seq, bq_w), kb.reshape(bsz, seq, 2 * bkv_w),
                          vb.reshape(bsz, seq, 2 * bkv_w)).reshape(bsz * seq, bq_w)
    x2, hn, aff = _out_proj([oa, ob], l0_w_out.astype(BF16), x.reshape(bsz * seq, d_model), l0_norm_ffn[None, :],
                            l0_router.T, bsz=bsz, seq=seq, tiled=False)
    x3 = _moe(x2, hn, aff, l0_w_gate, l0_w_up, l0_w_down, bsz=bsz, seq=seq)

    w1 = jnp.concatenate([l1_w_in[:, :cq_w], _dup_heads(l1_w_in[:, cq_w:cq_w + ckv_w], C_KV_HEADS),
                          _dup_heads(l1_w_in[:, cq_w + ckv_w:], C_KV_HEADS)], axis=1).astype(BF16)
    gains1 = jnp.concatenate([_tile_gain(l1_c_qnorm, scale), _tile_gain(l1_c_knorm)], axis=0)
    groups1 = ((0, cq_w, "norm", 0), (cq_w, 2 * ckv_w, "norm", 1), (cq_w + 2 * ckv_w, 2 * ckv_w, "plain", 0))
    qc, kc, vc = _norm_proj(x3, l1_norm_attn[None, :], w1, bd, gains1, cos_t, sin_t, groups1, seq=seq, tiled=True)
    bias_c = _band_bias(rel_bias, C_HEADS, C_RADIUS, C_RADIUS, C_RADIUS, 1)
    shp = lambda t: t.reshape(bsz, seq, t.shape[-1])
    oc = _band_attention(shp(qc), shp(kc), shp(vc), bias_c, kv_div=C_HEADS // C_KV_HEADS // 2, bias_mod=C_HEADS // 2,
                         qr=C_RADIUS, halo=C_RADIUS, sink=l1_sink.astype(F32)).reshape(bsz * seq, cq_w)
    x4, hn1, aff1 = _out_proj([oc], l1_w_out.astype(BF16), x3, l1_norm_ffn[None, :], l1_router.T,
                              bsz=bsz, seq=seq, tiled=True)
    x5 = _moe(x4, hn1, aff1, l1_w_gate, l1_w_up, l1_w_down, bsz=bsz, seq=seq)
    return x5.reshape(bsz, seq, nt, LANES).reshape(bsz, seq, d_model)
```

```python
import functools
import math

import jax
import jax.numpy as jnp
from jax import lax
from jax.experimental import pallas as pl
from jax.experimental.pallas import tpu as pltpu

F32 = jnp.float32
BF16 = jnp.bfloat16
I32 = jnp.int32

LANES = 128
HEAD_DIM = 64
NEG_INF = -1e30
EPS = 1e-6

A_HEADS = 8
A_BRANCHES = ((128, 1), (512, 4), (2048, 16))
B_HEADS = 8
B_KV_HEADS = 2
GRID_W = 64
ROPE_THETA = 10000.0
C_HEADS = 16
C_KV_HEADS = 4
C_RADIUS = 128
NUM_BUCKETS = 32
MAX_EXACT = 8
MAX_DISTANCE = 1024
N_EXPERTS = 16
CAPACITY_FACTOR = 2

MIB = 1024 * 1024
NT_DIMS = (((1,), (1,)), ((), ()))


def _params(semantics, vmem_mib):
    return pltpu.CompilerParams(dimension_semantics=semantics, vmem_limit_bytes=vmem_mib * MIB)


def _load_rows(x_ref, tm, nt, tiled):
    if not tiled:
        return x_ref[...]
    return jnp.concatenate([x_ref[pl.ds(s, tm, stride=nt), :] for s in range(nt)], axis=1)


def _store_rows_tiled(o_ref, val, tm, nt):
    for s in range(nt):
        o_ref[pl.ds(s, tm, stride=nt), :] = val[:, s * LANES:(s + 1) * LANES]


def _proj_body(x_ref, g_ref, w_ref, bd_ref, gains_ref, cos_ref, sin_ref, *out_refs, groups, tm, nt, tiled):
    x = _load_rows(x_ref, tm, nt, tiled)
    ms = jnp.mean(x * x, axis=-1, keepdims=True)
    xn = (x * lax.rsqrt(ms + EPS) * g_ref[...]).astype(BF16)
    lane = lax.broadcasted_iota(I32, (1, LANES), 1)
    first16 = (lane % 32) < 16
    bd = bd_ref[...]
    for (c0, width, kind, grow), o_ref in zip(groups, out_refs):
        y = jnp.dot(xn, w_ref[:, c0:c0 + width], preferred_element_type=F32)
        if kind == "plain":
            o_ref[...] = y.astype(o_ref.dtype)
            continue
        gain = gains_ref[grow:grow + 1, :]
        for j in range(width // LANES):
            yj = y[:, j * LANES:(j + 1) * LANES]
            sq = yj * yj
            hi = sq.astype(BF16)
            lo = (sq - hi.astype(F32)).astype(BF16)
            ss = jnp.dot(hi, bd, preferred_element_type=F32) + jnp.dot(lo, bd, preferred_element_type=F32)
            yn = yj * lax.rsqrt(ss * (1.0 / HEAD_DIM) + EPS) * gain
            if kind == "rope":
                partner = jnp.where(first16, pltpu.roll(yn, LANES - 16, 1), pltpu.roll(yn, 16, 1))
                yn = yn * cos_ref[...] + partner * sin_ref[...]
            o_ref[:, j * LANES:(j + 1) * LANES] = yn.astype(o_ref.dtype)


def _norm_proj(x, g, w, bd, gains, cos_t, sin_t, groups, *, seq, tiled, tm=512):
    d_model = w.shape[0]
    nt = d_model // LANES
    tokens = x.shape[0] // nt if tiled else x.shape[0]
    tm = min(tm, seq)
    nblk_seq = seq // tm
    x_spec = (pl.BlockSpec((tm * nt, LANES), lambda i: (i, 0)) if tiled
              else pl.BlockSpec((tm, d_model), lambda i: (i, 0)))
    const = lambda i: (0, 0)
    in_specs = [
        x_spec,
        pl.BlockSpec(g.shape, const),
        pl.BlockSpec(w.shape, const),
        pl.BlockSpec(bd.shape, const),
        pl.BlockSpec(gains.shape, const),
        pl.BlockSpec((tm, LANES), lambda i: (i % nblk_seq, 0)),
        pl.BlockSpec((tm, LANES), lambda i: (i % nblk_seq, 0)),
    ]
    out_shape = [jax.ShapeDtypeStruct((tokens, wd), BF16) for (_, wd, _, _) in groups]
    out_specs = [pl.BlockSpec((tm, wd), lambda i: (i, 0)) for (_, wd, _, _) in groups]
    return pl.pallas_call(
        functools.partial(_proj_body, groups=groups, tm=tm, nt=nt, tiled=tiled),
        grid=(tokens // tm,),
        in_specs=in_specs,
        out_specs=out_specs,
        out_shape=out_shape,
        compiler_params=_params(("parallel",), 48),
        name="norm_proj",
    )(x, g, w, bd, gains, cos_t, sin_t)


def _band_body(*refs, tq, qr, halo, length, has_state, final, has_sink):
    it = iter(refs)
    sink_ref = next(it) if has_sink else None
    q_ref, kp_ref, km_ref, kn_ref, vp_ref, vm_ref, vn_ref, bias_ref = (next(it) for _ in range(8))
    if has_state:
        m_in, l_in, a_in = next(it), next(it), next(it)
    if final:
        o_ref = next(it)
    else:
        m_out, l_out, a_out = next(it), next(it), next(it)

    cb = pl.program_id(1)
    blk = pl.program_id(2)
    win = qr + 2 * halo
    kext = jnp.concatenate([kp_ref[0], km_ref[0], kn_ref[0]], axis=0)
    vext = jnp.concatenate([vp_ref[0], vm_ref[0], vn_ref[0]], axis=0)
    lane = lax.broadcasted_iota(I32, (1, LANES), 1)
    lo = lane < HEAD_DIM
    col = lax.broadcasted_iota(I32, (1, win), 1)
    head_row = lax.broadcasted_iota(I32, (2 * qr, 1), 0) < qr
    bias = bias_ref[0]
    zero = jnp.zeros((), BF16)

    def split(tile):
        return jnp.concatenate([tile[:, 0:1], tile[:, HEAD_DIM:HEAD_DIM + 1]], axis=0)

    def merge(colv):
        return jnp.where(lo, colv[:qr], colv[qr:])

    for s in range(tq // qr):
        rows = slice(s * qr, (s + 1) * qr)
        qs = q_ref[0, rows, :]
        q2 = jnp.concatenate([jnp.where(lo, qs, zero), jnp.where(lo, zero, qs)], axis=0)
        kw = kext[s * qr:s * qr + win]
        vw = vext[s * qr:s * qr + win]
        sc = lax.dot_general(q2, kw, NT_DIMS, preferred_element_type=F32) + bias
        kpos = blk * tq + (s * qr - halo) + col
        sc = jnp.where((kpos >= 0) & (kpos < length), sc, NEG_INF)
        m_cur = jnp.max(sc, axis=1, keepdims=True)
        if has_state:
            m_prev = split(m_in[0, rows, :])
            l_prev = split(l_in[0, rows, :])
            m_new = jnp.maximum(m_prev, m_cur)
        else:
            m_new = m_cur
        if has_sink:
            sk = jnp.where(head_row, sink_ref[2 * cb], sink_ref[2 * cb + 1])
            m_new = jnp.maximum(m_new, sk)
        p = jnp.exp(sc - m_new)
        l_new = jnp.sum(p, axis=1, keepdims=True)
        pv = jnp.dot(p.astype(BF16), vw, preferred_element_type=F32)
        acc = merge(pv)
        if has_state:
            alpha = jnp.exp(m_prev - m_new)
            l_new = l_new + alpha * l_prev
            acc = acc + merge(alpha) * a_in[0, rows, :]
        if has_sink:
            l_new = l_new + jnp.exp(sk - m_new)
        if final:
            o_ref[0, rows, :] = (acc / merge(l_new)).astype(o_ref.dtype)
        else:
            m_out[0, rows, :] = jnp.broadcast_to(merge(m_new), (qr, LANES))
            l_out[0, rows, :] = jnp.broadcast_to(merge(l_new), (qr, LANES))
            a_out[0, rows, :] = acc


def _band_attention(q, k, v, bias, *, kv_div, bias_mod, qr, halo, state=None, final=True, sink=None, tq=512):
    bsz, length, qcols = q.shape
    ncb = qcols // LANES
    tq = min(tq, length)
    nq = length // tq
    per = tq // halo
    nh = length // halo
    win = qr + 2 * halo

    def qmap(b, cb, i, *_):
        return (b, i, cb)

    def kmain(b, cb, i, *_):
        return (b, i, cb // kv_div)

    def kprev(b, cb, i, *_):
        return (b, jnp.maximum(i * per - 1, 0), cb // kv_div)

    def knext(b, cb, i, *_):
        return (b, jnp.minimum((i + 1) * per, nh - 1), cb // kv_div)

    def bmap(b, cb, i, *_):
        return (cb % bias_mod, 0, 0)

    main = lambda m: pl.BlockSpec((1, tq, LANES), m)
    edge = lambda m: pl.BlockSpec((1, halo, LANES), m)
    in_specs = [main(qmap), edge(kprev), main(kmain), edge(knext), edge(kprev), main(kmain), edge(knext),
                pl.BlockSpec((1, 2 * qr, win), bmap)]
    args = [q, k, k, k, v, v, v, bias]
    if state is not None:
        in_specs += [main(qmap)] * 3
        args += list(state)
    if final:
        out_shape = jax.ShapeDtypeStruct(q.shape, BF16)
        out_specs = main(qmap)
    else:
        out_shape = [jax.ShapeDtypeStruct(q.shape, F32)] * 3
        out_specs = [main(qmap)] * 3
    has_sink = sink is not None
    grid_spec = pltpu.PrefetchScalarGridSpec(
        num_scalar_prefetch=1 if has_sink else 0,
        grid=(bsz, ncb, nq),
        in_specs=in_specs,
        out_specs=out_specs,
    )
    body = functools.partial(_band_body, tq=tq, qr=qr, halo=halo, length=length,
                             has_state=state is not None, final=final, has_sink=has_sink)
    call = pl.pallas_call(body, grid_spec=grid_spec, out_shape=out_shape,
                          compiler_params=_params(("parallel", "parallel", "parallel"), 48),
                          name="band_attention")
    return call(sink, *args) if has_sink else call(*args)


def _dense_body(q_ref, k_ref, v_ref, o_ref, q4_ref, s_a, s_b, p_a, p_b, al_a, al_b, m_ref, l_ref, acc_ref,
                *, tq, tk, nk, rb):
    lane = lax.broadcasted_iota(I32, (1, LANES), 1)
    lo = lane < HEAD_DIM
    zero = jnp.zeros((), BF16)
    for h in range(4):
        blk = q_ref[0, :, (h // 2) * LANES:(h // 2 + 1) * LANES]
        q4_ref[h * tq:(h + 1) * tq, :] = jnp.where(lo, blk, zero) if h % 2 == 0 else jnp.where(lo, zero, blk)
    m_ref[...] = jnp.full(m_ref.shape, NEG_INF, F32)
    l_ref[...] = jnp.zeros(l_ref.shape, F32)
    acc_ref[...] = jnp.zeros(acc_ref.shape, F32)
    p_b[...] = jnp.zeros(p_b.shape, BF16)
    al_b[...] = jnp.ones(al_b.shape, F32)
    nct = tk // LANES

    def scores(j, s_ref):
        start = pl.multiple_of(jnp.minimum(j, nk - 1) * tk, tk)
        s_ref[...] = lax.dot_general(q4_ref[...], k_ref[0, pl.ds(start, tk), :], NT_DIMS, preferred_element_type=F32)

    def weighted_values(j, p_ref, al_ref):
        start = pl.multiple_of(jnp.maximum(j, 0) * tk, tk)
        acc_ref[...] = al_ref[...] * acc_ref[...] + jnp.dot(p_ref[...], v_ref[0, pl.ds(start, tk), :],
                                                            preferred_element_type=F32)

    def softmax_update(s_ref, p_ref, al_ref):
        for r in range(4 * tq // rb):
            rows = slice(r * rb, (r + 1) * rb)
            tiles = [s_ref[rows, c * LANES:(c + 1) * LANES] for c in range(nct)]
            mt = functools.reduce(jnp.maximum, tiles)
            m_prev = m_ref[rows, :]
            m_new = jnp.maximum(m_prev, jnp.max(mt, axis=1, keepdims=True))
            alpha = jnp.exp(m_prev - m_new)
            ps = [jnp.exp(t - m_new) for t in tiles]
            l_ref[rows, :] = alpha * l_ref[rows, :] + functools.reduce(jnp.add, ps)
            for c in range(nct):
                p_ref[rows, c * LANES:(c + 1) * LANES] = ps[c].astype(BF16)
            al_ref[rows, :] = alpha
            m_ref[rows, :] = m_new

    scores(0, s_a)

    def pair(i, carry):
        j = 2 * i
        scores(j + 1, s_b)
        weighted_values(j - 1, p_b, al_b)
        softmax_update(s_a, p_a, al_a)
        scores(j + 2, s_a)
        weighted_values(j, p_a, al_a)
        softmax_update(s_b, p_b, al_b)
        return carry

    lax.fori_loop(0, nk // 2, pair, 0)
    weighted_values(nk - 1, p_b, al_b)
    o = acc_ref[...] / jnp.sum(l_ref[...], axis=1, keepdims=True)
    for hp in range(2):
        o_ref[0, :, hp * LANES:(hp + 1) * LANES] = jnp.where(
            lo, o[(2 * hp) * tq:(2 * hp + 1) * tq], o[(2 * hp + 1) * tq:(2 * hp + 2) * tq]).astype(o_ref.dtype)


def _dense_attention(q, k, v, *, tq=256, tk=512, rb=32):
    bsz, seq, _ = q.shape
    tq = min(tq, seq)
    tk = min(tk, seq // 2)
    assert seq % (2 * tk) == 0 and seq % tq == 0
    ngroups = k.shape[2] // LANES
    return pl.pallas_call(
        functools.partial(_dense_body, tq=tq, tk=tk, nk=seq // tk, rb=min(rb, 4 * tq)),
        grid=(bsz, ngroups, seq // tq),
        in_specs=[
            pl.BlockSpec((1, tq, 2 * LANES), lambda b, g, i: (b, i, g)),
            pl.BlockSpec((1, seq, LANES), lambda b, g, i: (b, 0, g)),
            pl.BlockSpec((1, seq, LANES), lambda b, g, i: (b, 0, g)),
        ],
        out_specs=pl.BlockSpec((1, tq, 2 * LANES), lambda b, g, i: (b, i, g)),
        out_shape=jax.ShapeDtypeStruct(q.shape, BF16),
        scratch_shapes=(
            [pltpu.VMEM((4 * tq, LANES), BF16)]
            + [pltpu.VMEM((4 * tq, tk), F32)] * 2
            + [pltpu.VMEM((4 * tq, tk), BF16)] * 2
            + [pltpu.VMEM((4 * tq, LANES), F32)] * 5
        ),
        compiler_params=_params(("parallel", "parallel", "parallel"), 48),
        name="dense_attention",
    )(q, k, v)


def _outproj_body(*refs, n_in, tm, nt, tiled):
    o_refs = refs[:n_in]
    w_ref, x_ref, g_ref, wr_ref, x2_ref, hn_ref, aff_ref = refs[n_in:]
    y = None
    c0 = 0
    for o_ref in o_refs:
        wd = o_ref.shape[1]
        part = jnp.dot(o_ref[...], w_ref[c0:c0 + wd, :], preferred_element_type=F32)
        y = part if y is None else y + part
        c0 += wd
    x2 = _load_rows(x_ref, tm, nt, tiled) + y
    ms = jnp.mean(x2 * x2, axis=-1, keepdims=True)
    hn = x2 * lax.rsqrt(ms + EPS) * g_ref[...]
    _store_rows_tiled(x2_ref, x2, tm, nt)
    _store_rows_tiled(hn_ref, hn, tm, nt)
    wr = wr_ref[...]
    wr_hi = wr.astype(BF16)
    wr_lo = (wr - wr_hi.astype(F32)).astype(BF16)
    hn_hi = hn.astype(BF16)
    hn_lo = (hn - hn_hi.astype(F32)).astype(BF16)
    lt = (lax.dot_general(wr_hi, hn_hi, NT_DIMS, preferred_element_type=F32)
          + lax.dot_general(wr_lo, hn_hi, NT_DIMS, preferred_element_type=F32)
          + lax.dot_general(wr_hi, hn_lo, NT_DIMS, preferred_element_type=F32))
    ex = jnp.exp(lt - jnp.max(lt, axis=0, keepdims=True))
    aff_ref[0] = ex / jnp.sum(ex, axis=0, keepdims=True)


def _out_proj(o_parts, w, x, g, wr_t, *, bsz, seq, tiled, tm=512):
    d_model = w.shape[1]
    nt = d_model // LANES
    tokens = bsz * seq
    tm = min(tm, seq)
    nblk_seq = seq // tm
    nexp = wr_t.shape[0]
    const = lambda i: (0, 0)
    x_spec = (pl.BlockSpec((tm * nt, LANES), lambda i: (i, 0)) if tiled
              else pl.BlockSpec((tm, d_model), lambda i: (i, 0)))
    in_specs = [pl.BlockSpec((tm, o.shape[1]), lambda i: (i, 0)) for o in o_parts]
    in_specs += [pl.BlockSpec(w.shape, const), x_spec, pl.BlockSpec(g.shape, const), pl.BlockSpec(wr_t.shape, const)]
    tiled_spec = pl.BlockSpec((tm * nt, LANES), lambda i: (i, 0))
    return pl.pallas_call(
        functools.partial(_outproj_body, n_in=len(o_parts), tm=tm, nt=nt, tiled=tiled),
        grid=(tokens // tm,),
        in_specs=in_specs,
        out_specs=[tiled_spec, tiled_spec,
                   pl.BlockSpec((1, nexp, tm), lambda i: (i // nblk_seq, 0, i % nblk_seq))],
        out_shape=[jax.ShapeDtypeStruct((tokens * nt, LANES), F32),
                   jax.ShapeDtypeStruct((tokens * nt, LANES), F32),
                   jax.ShapeDtypeStruct((bsz, nexp, seq), F32)],
        compiler_params=_params(("parallel",), 48),
        name="out_proj_router",
    )(*o_parts, w, x, g, wr_t)


CUMSUM_BLOCK = 256
INDEX_CHUNK = 128


def _lane_cumsum(x, tri):
    outs = []
    carry = jnp.zeros((x.shape[0], 1), F32)
    for k in range(x.shape[1] // CUMSUM_BLOCK):
        blk = x[:, k * CUMSUM_BLOCK:(k + 1) * CUMSUM_BLOCK].astype(BF16)
        c = jnp.dot(blk, tri, preferred_element_type=F32) + carry
        outs.append(c)
        carry = c[:, CUMSUM_BLOCK - 1:CUMSUM_BLOCK]
    return jnp.concatenate(outs, axis=1)


def _route_body(aff_ref, idx_ref, cs_ref, *, seq, cap, nexp):
    e = pl.program_id(1)

    @pl.when(e == 0)
    def _():
        aff = aff_ref[0]
        bits = pltpu.bitcast(aff, I32)

        def search(_, c):
            lo, hi = c
            mid = lo + ((hi - lo + 1) >> 1)
            cnt = jnp.sum(jnp.where(bits >= mid, 1.0, 0.0), axis=1, keepdims=True)
            ok = cnt >= cap
            return jnp.where(ok, mid, lo), jnp.where(ok, hi, mid - 1)

        lo0 = jnp.zeros((nexp, 1), I32)
        hi0 = jnp.full((nexp, 1), 0x7F800000, I32)
        thr, _ = lax.fori_loop(0, 32, search, (lo0, hi0))
        gt = bits > thr
        eq = bits == thr
        r = lax.broadcasted_iota(I32, (CUMSUM_BLOCK, CUMSUM_BLOCK), 0)
        c = lax.broadcasted_iota(I32, (CUMSUM_BLOCK, CUMSUM_BLOCK), 1)
        tri = jnp.where(r <= c, 1.0, 0.0).astype(BF16)
        need = cap - jnp.sum(jnp.where(gt, 1.0, 0.0), axis=1, keepdims=True)
        eq_rank = _lane_cumsum(jnp.where(eq, 1.0, 0.0), tri)
        sel = gt | (eq & (eq_rank <= need))
        cs = _lane_cumsum(jnp.where(sel, 1.0, 0.0), tri)
        for ee in range(nexp):
            cs_ref[ee] = cs[ee:ee + 1, :]

    ones = jnp.ones((8, LANES), BF16)
    for cc in range(cap // INDEX_CHUNK):
        slot = (lax.broadcasted_iota(I32, (INDEX_CHUNK, LANES), 0) + cc * INDEX_CHUNK).astype(F32)

        def tile(k, part):
            row = cs_ref[e, :, pl.ds(pl.multiple_of(k * LANES, LANES), LANES)]
            return part + jnp.where(row <= slot, 1.0, 0.0)

        part = lax.fori_loop(0, seq // LANES, tile, jnp.zeros((INDEX_CHUNK, LANES), F32))
        cnt = lax.dot_general(ones, part.astype(BF16), NT_DIMS, preferred_element_type=F32)
        idx_ref[0, :, cc * INDEX_CHUNK:(cc + 1) * INDEX_CHUNK] = cnt[0:1].astype(I32)


def _route(aff_t, cap):
    bsz, nexp, seq = aff_t.shape
    return pl.pallas_call(
        functools.partial(_route_body, seq=seq, cap=cap, nexp=nexp),
        grid=(bsz, nexp),
        in_specs=[pl.BlockSpec((1, nexp, seq), lambda b, e: (b, 0, 0))],
        out_specs=pl.BlockSpec((1, 1, cap), lambda b, e: (b * nexp + e, 0, 0)),
        out_shape=jax.ShapeDtypeStruct((bsz * nexp, 1, cap), I32),
        scratch_shapes=[pltpu.VMEM((nexp, 1, seq), F32)],
        compiler_params=_params(("parallel", "arbitrary"), 48),
        name="route",
    )(aff_t)


ROW_UNROLL = 8


def _gather_body(idx_ref, h_hbm, x_ref, hbuf, xbuf, sem, *, cap, nt):
    b = pl.program_id(0)
    e = pl.program_id(1)

    @pl.when(e == 0)
    def _():
        cp = pltpu.make_async_copy(h_hbm.at[b], hbuf, sem)
        cp.start()
        cp.wait()

    def rows(i, carry):
        for u in range(ROW_UNROLL):
            c = i * ROW_UNROLL + u
            t = idx_ref[0, 0, c]
            xbuf[pl.ds(pl.multiple_of(c * nt, nt), nt), :] = hbuf[pl.ds(pl.multiple_of(t * nt, nt), nt), :]
        return carry

    lax.fori_loop(0, cap // ROW_UNROLL, rows, 0)
    for s in range(nt):
        x_ref[0, :, s * LANES:(s + 1) * LANES] = xbuf[pl.ds(s, cap, stride=nt), :].astype(x_ref.dtype)


def _gather(idx, h_tiled, *, bsz, nexp, seq, cap, nt):
    return pl.pallas_call(
        functools.partial(_gather_body, cap=cap, nt=nt),
        grid=(bsz, nexp),
        in_specs=[pl.BlockSpec((1, 1, cap), lambda b, e: (b * nexp + e, 0, 0), memory_space=pltpu.SMEM),
                  pl.BlockSpec(memory_space=pl.ANY)],
        out_specs=pl.BlockSpec((1, cap, nt * LANES), lambda b, e: (e, b, 0)),
        out_shape=jax.ShapeDtypeStruct((nexp, bsz * cap, nt * LANES), BF16),
        scratch_shapes=[pltpu.VMEM((seq * nt, LANES), F32), pltpu.VMEM((cap * nt, LANES), F32),
                        pltpu.SemaphoreType.DMA(())],
        compiler_params=_params(("arbitrary", "arbitrary"), 56),
        name="moe_gather",
    )(idx, h_tiled)


def _ffn_body(x_ref, wg_ref, wu_ref, wd_ref, y_ref, acc_ref, *, cap, nt):
    f = pl.program_id(2)
    x = x_ref[0]
    g = jnp.dot(x, wg_ref[0].astype(BF16), preferred_element_type=F32)
    u = jnp.dot(x, wu_ref[0].astype(BF16), preferred_element_type=F32)
    act = (g * (1.0 / (1.0 + jnp.exp(-g))) * u).astype(BF16)
    y = jnp.dot(act, wd_ref[0].astype(BF16), preferred_element_type=F32)

    @pl.when(f == 0)
    def _():
        acc_ref[...] = y

    @pl.when(f > 0)
    def _():
        acc_ref[...] += y

    @pl.when(f == pl.num_programs(2) - 1)
    def _():
        _store_rows_tiled(y_ref.at[0], acc_ref[...], cap, nt)


def _ffn(x, wg, wu, wd, *, bsz, cap, tf=512):
    nexp, d_model, d_exp = wg.shape
    nt = d_model // LANES
    tf = min(tf, d_exp)
    return pl.pallas_call(
        functools.partial(_ffn_body, cap=cap, nt=nt),
        grid=(nexp, bsz, d_exp // tf),
        in_specs=[pl.BlockSpec((1, cap, d_model), lambda e, b, f: (e, b, 0)),
                  pl.BlockSpec((1, d_model, tf), lambda e, b, f: (e, 0, f)),
                  pl.BlockSpec((1, d_model, tf), lambda e, b, f: (e, 0, f)),
                  pl.BlockSpec((1, tf, d_model), lambda e, b, f: (e, f, 0))],
        out_specs=pl.BlockSpec((1, cap * nt, LANES), lambda e, b, f: (b * nexp + e, 0, 0)),
        out_shape=jax.ShapeDtypeStruct((bsz * nexp, cap * nt, LANES), F32),
        scratch_shapes=[pltpu.VMEM((cap, d_model), F32)],
        compiler_params=_params(("parallel", "parallel", "arbitrary"), 56),
        name="moe_ffn",
    )(x, wg, wu, wd)


def _combine_body(idx_ref, aff_ref, x_hbm, y_ref, o_hbm, acc, sem, *, cap, nt, nexp):
    b = pl.program_id(0)
    e = pl.program_id(1)

    @pl.when(e == 0)
    def _():
        cp = pltpu.make_async_copy(x_hbm.at[b], acc, sem)
        cp.start()
        cp.wait()

    def rows(i, carry):
        dst, vals = [], []
        for u in range(ROW_UNROLL):
            c = i * ROW_UNROLL + u
            t = idx_ref[0, 0, c]
            gate = aff_ref[0, 0, t]
            d = pl.ds(pl.multiple_of(t * nt, nt), nt)
            dst.append(d)
            vals.append(acc[d, :] + gate * y_ref[0, pl.ds(pl.multiple_of(c * nt, nt), nt), :])
        for d, val in zip(dst, vals):
            acc[d, :] = val
        return carry

    lax.fori_loop(0, cap // ROW_UNROLL, rows, 0)

    @pl.when(e == nexp - 1)
    def _():
        cp = pltpu.make_async_copy(acc, o_hbm.at[b], sem)
        cp.start()
        cp.wait()


def _combine(idx, aff_rows, x_tiled, y_tiled, *, bsz, nexp, seq, cap, nt):
    return pl.pallas_call(
        functools.partial(_combine_body, cap=cap, nt=nt, nexp=nexp),
        grid=(bsz, nexp),
        in_specs=[pl.BlockSpec((1, 1, cap), lambda b, e: (b * nexp + e, 0, 0), memory_space=pltpu.SMEM),
                  pl.BlockSpec((1, 1, seq), lambda b, e: (b * nexp + e, 0, 0), memory_space=pltpu.SMEM),
                  pl.BlockSpec(memory_space=pl.ANY),
                  pl.BlockSpec((1, cap * nt, LANES), lambda b, e: (b * nexp + e, 0, 0))],
        out_specs=pl.BlockSpec(memory_space=pl.ANY),
        out_shape=jax.ShapeDtypeStruct(x_tiled.shape, F32),
        scratch_shapes=[pltpu.VMEM((seq * nt, LANES), F32), pltpu.SemaphoreType.DMA(())],
        compiler_params=_params(("arbitrary", "arbitrary"), 56),
        name="moe_combine",
    )(idx, aff_rows, x_tiled, y_tiled)


def _moe(x2_t, hn_t, aff_t, wg, wu, wd, *, bsz, seq):
    nexp = aff_t.shape[1]
    nt = wg.shape[1] // LANES
    cap = CAPACITY_FACTOR * seq // nexp
    idx = _route(aff_t, cap)
    xg = _gather(idx, hn_t.reshape(bsz, seq * nt, LANES), bsz=bsz, nexp=nexp, seq=seq, cap=cap, nt=nt)
    y = _ffn(xg, wg, wu, wd, bsz=bsz, cap=cap)
    out = _combine(idx, aff_t.reshape(bsz * nexp, 1, seq), x2_t.reshape(bsz, seq * nt, LANES), y,
                   bsz=bsz, nexp=nexp, seq=seq, cap=cap, nt=nt)
    return out.reshape(bsz * seq * nt, LANES)


def _t5_bucket(rel):
    half = NUM_BUCKETS // 2
    ret = jnp.where(rel > 0, half, 0)
    n = jnp.abs(rel)
    nf = jnp.maximum(n, 1).astype(F32)
    large = MAX_EXACT + (jnp.log(nf / MAX_EXACT) / math.log(MAX_DISTANCE / MAX_EXACT)
                         * (half - MAX_EXACT)).astype(I32)
    large = jnp.minimum(large, half - 1)
    return ret + jnp.where(n < MAX_EXACT, n, large)


def _band_bias(rel_bias, nheads, qr, halo, radius, dil):
    rel = jnp.arange(qr + 2 * halo)[None, :] - halo - jnp.arange(qr)[:, None]
    onehot = (_t5_bucket(rel * dil)[None] == jnp.arange(NUM_BUCKETS)[:, None, None]).astype(F32)
    tile = jnp.sum(rel_bias[:nheads].astype(F32)[:, :, None, None] * onehot[None], axis=1)
    tile = jnp.where((jnp.abs(rel) <= radius)[None], tile, NEG_INF)
    return tile.reshape(nheads // 2, 2 * qr, qr + 2 * halo)


def _tile_gain(g, scale=1.0):
    return jnp.tile(g.astype(F32) * scale, LANES // HEAD_DIM)[None, :]


def _dup_heads(w, nheads):
    d = w.shape[0]
    w4 = w.reshape(d, nheads, 1, HEAD_DIM)
    return jnp.broadcast_to(w4, (d, nheads, 2, HEAD_DIM)).reshape(d, nheads * 2 * HEAD_DIM)


def _rope_tables(seq):
    half = HEAD_DIM // 2
    inv = ROPE_THETA ** (-jnp.arange(0, half, 2, dtype=F32) / half)
    t = jnp.arange(seq)
    row = (t // GRID_W).astype(F32)[:, None] * inv
    colp = (t % GRID_W).astype(F32)[:, None] * inv
    cos64 = jnp.concatenate([jnp.cos(row), jnp.cos(row), jnp.cos(colp), jnp.cos(colp)], axis=1)
    sin64 = jnp.concatenate([-jnp.sin(row), jnp.sin(row), -jnp.sin(colp), jnp.sin(colp)], axis=1)
    return jnp.tile(cos64, (1, 2)), jnp.tile(sin64, (1, 2))


def _block_diag_ones():
    r = jnp.arange(LANES)
    return (r[:, None] // HEAD_DIM == r[None, :] // HEAD_DIM).astype(BF16)


def kernel(x, rel_bias, l0_norm_attn, l0_w_in, l0_a_qnorm, l0_a_knorm, l0_b_qnorm, l0_b_knorm, l0_w_out,
           l0_norm_ffn, l0_router, l0_w_gate, l0_w_up, l0_w_down, l1_norm_attn, l1_w_in, l1_c_qnorm, l1_c_knorm,
           l1_sink, l1_w_out, l1_norm_ffn, l1_router, l1_w_gate, l1_w_up, l1_w_down):
    bsz, seq, d_model = x.shape
    nt = d_model // LANES
    scale = 1.0 / math.sqrt(HEAD_DIM)
    a_w = A_HEADS * HEAD_DIM
    bq_w = B_HEADS * HEAD_DIM
    bkv_w = B_KV_HEADS * HEAD_DIM
    cq_w = C_HEADS * HEAD_DIM
    ckv_w = C_KV_HEADS * HEAD_DIM
    bd = _block_diag_ones()
    cos_t, sin_t = _rope_tables(seq)

    c = 3 * a_w + bq_w
    w0 = jnp.concatenate([l0_w_in[:, :c], _dup_heads(l0_w_in[:, c:c + bkv_w], B_KV_HEADS),
                          _dup_heads(l0_w_in[:, c + bkv_w:], B_KV_HEADS)], axis=1).astype(BF16)
    gains0 = jnp.concatenate([_tile_gain(l0_a_qnorm, scale), _tile_gain(l0_a_knorm),
                              _tile_gain(l0_b_qnorm, scale), _tile_gain(l0_b_knorm)], axis=0)
    groups0 = ((0, a_w, "norm", 0), (a_w, a_w, "norm", 1), (2 * a_w, a_w, "plain", 0),
               (3 * a_w, bq_w, "rope", 2), (c, 2 * bkv_w, "rope", 3), (c + 2 * bkv_w, 2 * bkv_w, "plain", 0))
    qa, ka, va, qb, kb, vb = _norm_proj(x.reshape(bsz * seq, d_model), l0_norm_attn[None, :], w0, bd, gains0,
                                        cos_t, sin_t, groups0, seq=seq, tiled=False)
    state = None
    for n, (window, dil) in enumerate(A_BRANCHES):
        radius = window // (2 * dil)
        view = lambda t: t.reshape(bsz, seq // dil, dil * t.shape[-1])
        bias = _band_bias(rel_bias, A_HEADS, 2 * radius, radius, radius, dil)
        last = n == len(A_BRANCHES) - 1
        res = _band_attention(view(qa), view(ka), view(va), bias, kv_div=1, bias_mod=A_HEADS // 2,
                              qr=2 * radius, halo=radius, state=None if state is None else tuple(view(t) for t in state),
                              final=last)
        if last:
            oa = res.reshape(bsz * seq, a_w)
        else:
            state = tuple(t.reshape(bsz * seq, a_w) for t in res)
    ob = _dense_attention(qb.reshape(bsz, seq, bq_w), kb.reshape(bsz, seq, 2 * bkv_w),
                          vb.reshape(bsz, seq, 2 * bkv_w)).reshape(bsz * seq, bq_w)
    x2, hn, aff = _out_proj([oa, ob], l0_w_out.astype(BF16), x.reshape(bsz * seq, d_model), l0_norm_ffn[None, :],
                            l0_router.T, bsz=bsz, seq=seq, tiled=False)
    x3 = _moe(x2, hn, aff, l0_w_gate, l0_w_up, l0_w_down, bsz=bsz, seq=seq)

    w1 = jnp.concatenate([l1_w_in[:, :cq_w], _dup_heads(l1_w_in[:, cq_w:cq_w + ckv_w], C_KV_HEADS),
                          _dup_heads(l1_w_in[:, cq_w + ckv_w:], C_KV_HEADS)], axis=1).astype(BF16)
    gains1 = jnp.concatenate([_tile_gain(l1_c_qnorm, scale), _tile_gain(l1_c_knorm)], axis=0)
    groups1 = ((0, cq_w, "norm", 0), (cq_w, 2 * ckv_w, "norm", 1), (cq_w + 2 * ckv_w, 2 * ckv_w, "plain", 0))
    qc, kc, vc = _norm_proj(x3, l1_norm_attn[None, :], w1, bd, gains1, cos_t, sin_t, groups1, seq=seq, tiled=True)
    bias_c = _band_bias(rel_bias, C_HEADS, C_RADIUS, C_RADIUS, C_RADIUS, 1)
    shp = lambda t: t.reshape(bsz, seq, t.shape[-1])
    oc = _band_attention(shp(qc), shp(kc), shp(vc), bias_c, kv_div=C_HEADS // C_KV_HEADS // 2, bias_mod=C_HEADS // 2,
                         qr=C_RADIUS, halo=C_RADIUS, sink=l1_sink.astype(F32)).reshape(bsz * seq, cq_w)
    x4, hn1, aff1 = _out_proj([oc], l1_w_out.astype(BF16), x3, l1_norm_ffn[None, :], l1_router.T,
                              bsz=bsz, seq=seq, tiled=True)
    x5 = _moe(x4, hn1, aff1, l1_w_gate, l1_w_up, l1_w_down, bsz=bsz, seq=seq)
    return x5.reshape(bsz, seq, nt, LANES).reshape(bsz, seq, d_model)
```

```python
import functools
import math

import jax
import jax.numpy as jnp
from jax import lax
from jax.experimental import pallas as pl
from jax.experimental.pallas import tpu as pltpu

F32 = jnp.float32
BF16 = jnp.bfloat16
I32 = jnp.int32

LANES = 128
MXU_DEPTH = 256
HEAD_DIM = 64
NEG_INF = -1e30
EPS = 1e-6

A_HEADS = 8
A_BRANCHES = ((128, 1), (512, 4), (2048, 16))
B_HEADS = 8
B_KV_HEADS = 2
GRID_W = 64
ROPE_THETA = 10000.0
C_HEADS = 16
C_KV_HEADS = 4
C_RADIUS = 128
NUM_BUCKETS = 32
MAX_EXACT = 8
MAX_DISTANCE = 1024
N_EXPERTS = 16
CAPACITY_FACTOR = 2

MIB = 1024 * 1024
NT_DIMS = (((1,), (1,)), ((), ()))


def _params(semantics, vmem_mib):
    return pltpu.CompilerParams(dimension_semantics=semantics, vmem_limit_bytes=vmem_mib * MIB)


def _load_rows(x_ref, tm, nt, tiled):
    if not tiled:
        return x_ref[...]
    return jnp.concatenate([x_ref[pl.ds(s, tm, stride=nt), :] for s in range(nt)], axis=1)


def _store_rows_tiled(o_ref, val, tm, nt):
    for s in range(nt):
        o_ref[pl.ds(s, tm, stride=nt), :] = val[:, s * LANES:(s + 1) * LANES]


def _proj_body(x_ref, g_ref, w_ref, bd_ref, gains_ref, cos_ref, sin_ref, *out_refs, groups, tm, nt, tiled):
    x = _load_rows(x_ref, tm, nt, tiled)
    ms = jnp.mean(x * x, axis=-1, keepdims=True)
    xn = (x * lax.rsqrt(ms + EPS) * g_ref[...]).astype(BF16)
    lane = lax.broadcasted_iota(I32, (1, LANES), 1)
    first16 = (lane % 32) < 16
    bd = bd_ref[...]
    for (c0, width, kind, grow), o_ref in zip(groups, out_refs):
        y = jnp.dot(xn, w_ref[:, c0:c0 + width], preferred_element_type=F32)
        if kind == "plain":
            o_ref[...] = y.astype(o_ref.dtype)
            continue
        if kind == "value_ones":
            for j in range(width // LANES):
                o_ref[:, j * LANES:(j + 1) * LANES] = jnp.where(
                    lane < HEAD_DIM, y[:, j * LANES:(j + 1) * LANES], 1.0).astype(o_ref.dtype)
            continue
        gain = gains_ref[grow:grow + 1, :]
        for j in range(width // LANES):
            yj = y[:, j * LANES:(j + 1) * LANES]
            sq = yj * yj
            hi = sq.astype(BF16)
            lo = (sq - hi.astype(F32)).astype(BF16)
            ss = jnp.dot(hi, bd, preferred_element_type=F32) + jnp.dot(lo, bd, preferred_element_type=F32)
            yn = yj * lax.rsqrt(ss * (1.0 / HEAD_DIM) + EPS) * gain
            if kind == "rope":
                partner = jnp.where(first16, pltpu.roll(yn, LANES - 16, 1), pltpu.roll(yn, 16, 1))
                yn = yn * cos_ref[...] + partner * sin_ref[...]
            o_ref[:, j * LANES:(j + 1) * LANES] = yn.astype(o_ref.dtype)


def _norm_proj(x, g, w, bd, gains, cos_t, sin_t, groups, *, seq, tiled, tm=512):
    d_model = w.shape[0]
    nt = d_model // LANES
    tokens = x.shape[0] // nt if tiled else x.shape[0]
    tm = min(tm, seq)
    nblk_seq = seq // tm
    x_spec = (pl.BlockSpec((tm * nt, LANES), lambda i: (i, 0)) if tiled
              else pl.BlockSpec((tm, d_model), lambda i: (i, 0)))
    const = lambda i: (0, 0)
    in_specs = [
        x_spec,
        pl.BlockSpec(g.shape, const),
        pl.BlockSpec(w.shape, const),
        pl.BlockSpec(bd.shape, const),
        pl.BlockSpec(gains.shape, const),
        pl.BlockSpec((tm, LANES), lambda i: (i % nblk_seq, 0)),
        pl.BlockSpec((tm, LANES), lambda i: (i % nblk_seq, 0)),
    ]
    out_shape = [jax.ShapeDtypeStruct((tokens, wd), BF16) for (_, wd, _, _) in groups]
    out_specs = [pl.BlockSpec((tm, wd), lambda i: (i, 0)) for (_, wd, _, _) in groups]
    return pl.pallas_call(
        functools.partial(_proj_body, groups=groups, tm=tm, nt=nt, tiled=tiled),
        grid=(tokens // tm,),
        in_specs=in_specs,
        out_specs=out_specs,
        out_shape=out_shape,
        compiler_params=_params(("parallel",), 48),
        name="norm_proj",
    )(x, g, w, bd, gains, cos_t, sin_t)


def _band_body(*refs, tq, qr, halo, length, has_state, final, has_sink):
    it = iter(refs)
    sink_ref = next(it) if has_sink else None
    q_ref, kp_ref, km_ref, kn_ref, vp_ref, vm_ref, vn_ref, bias_ref = (next(it) for _ in range(8))
    if has_state:
        m_in, l_in, a_in = next(it), next(it), next(it)
    if final:
        o_ref = next(it)
    else:
        m_out, l_out, a_out = next(it), next(it), next(it)

    cb = pl.program_id(1)
    blk = pl.program_id(2)
    win = qr + 2 * halo
    kext = jnp.concatenate([kp_ref[0], km_ref[0], kn_ref[0]], axis=0)
    vext = jnp.concatenate([vp_ref[0], vm_ref[0], vn_ref[0]], axis=0)
    lane = lax.broadcasted_iota(I32, (1, LANES), 1)
    lo = lane < HEAD_DIM
    col = lax.broadcasted_iota(I32, (1, win), 1)
    head_row = lax.broadcasted_iota(I32, (2 * qr, 1), 0) < qr
    bias = bias_ref[0]
    zero = jnp.zeros((), BF16)

    def split(tile):
        return jnp.concatenate([tile[:, 0:1], tile[:, HEAD_DIM:HEAD_DIM + 1]], axis=0)

    def merge(colv):
        return jnp.where(lo, colv[:qr], colv[qr:])

    for s in range(tq // qr):
        rows = slice(s * qr, (s + 1) * qr)
        qs = q_ref[0, rows, :]
        q2 = jnp.concatenate([jnp.where(lo, qs, zero), jnp.where(lo, zero, qs)], axis=0)
        kw = kext[s * qr:s * qr + win]
        vw = vext[s * qr:s * qr + win]
        sc = lax.dot_general(q2, kw, NT_DIMS, preferred_element_type=F32) + bias
        kpos = blk * tq + (s * qr - halo) + col
        sc = jnp.where((kpos >= 0) & (kpos < length), sc, NEG_INF)
        m_cur = jnp.max(sc, axis=1, keepdims=True)
        if has_state:
            m_prev = split(m_in[0, rows, :])
            l_prev = split(l_in[0, rows, :])
            m_new = jnp.maximum(m_prev, m_cur)
        else:
            m_new = m_cur
        if has_sink:
            sk = jnp.where(head_row, sink_ref[2 * cb], sink_ref[2 * cb + 1])
            m_new = jnp.maximum(m_new, sk)
        p = jnp.exp(sc - m_new)
        l_new = jnp.sum(p, axis=1, keepdims=True)
        pv = jnp.dot(p.astype(BF16), vw, preferred_element_type=F32)
        acc = merge(pv)
        if has_state:
            alpha = jnp.exp(m_prev - m_new)
            l_new = l_new + alpha * l_prev
            acc = acc + merge(alpha) * a_in[0, rows, :]
        if has_sink:
            l_new = l_new + jnp.exp(sk - m_new)
        if final:
            o_ref[0, rows, :] = (acc / merge(l_new)).astype(o_ref.dtype)
        else:
            m_out[0, rows, :] = jnp.broadcast_to(merge(m_new), (qr, LANES))
            l_out[0, rows, :] = jnp.broadcast_to(merge(l_new), (qr, LANES))
            a_out[0, rows, :] = acc


def _band_attention(q, k, v, bias, *, kv_div, bias_mod, qr, halo, state=None, final=True, sink=None, tq=512):
    bsz, length, qcols = q.shape
    ncb = qcols // LANES
    tq = min(tq, length)
    nq = length // tq
    per = tq // halo
    nh = length // halo
    win = qr + 2 * halo

    def qmap(b, cb, i, *_):
        return (b, i, cb)

    def kmain(b, cb, i, *_):
        return (b, i, cb // kv_div)

    def kprev(b, cb, i, *_):
        return (b, jnp.maximum(i * per - 1, 0), cb // kv_div)

    def knext(b, cb, i, *_):
        return (b, jnp.minimum((i + 1) * per, nh - 1), cb // kv_div)

    def bmap(b, cb, i, *_):
        return (cb % bias_mod, 0, 0)

    main = lambda m: pl.BlockSpec((1, tq, LANES), m)
    edge = lambda m: pl.BlockSpec((1, halo, LANES), m)
    in_specs = [main(qmap), edge(kprev), main(kmain), edge(knext), edge(kprev), main(kmain), edge(knext),
                pl.BlockSpec((1, 2 * qr, win), bmap)]
    args = [q, k, k, k, v, v, v, bias]
    if state is not None:
        in_specs += [main(qmap)] * 3
        args += list(state)
    if final:
        out_shape = jax.ShapeDtypeStruct(q.shape, BF16)
        out_specs = main(qmap)
    else:
        out_shape = [jax.ShapeDtypeStruct(q.shape, F32)] * 3
        out_specs = [main(qmap)] * 3
    has_sink = sink is not None
    grid_spec = pltpu.PrefetchScalarGridSpec(
        num_scalar_prefetch=1 if has_sink else 0,
        grid=(bsz, ncb, nq),
        in_specs=in_specs,
        out_specs=out_specs,
    )
    body = functools.partial(_band_body, tq=tq, qr=qr, halo=halo, length=length,
                             has_state=state is not None, final=final, has_sink=has_sink)
    call = pl.pallas_call(body, grid_spec=grid_spec, out_shape=out_shape,
                          compiler_params=_params(("parallel", "parallel", "parallel"), 48),
                          name="band_attention")
    return call(sink, *args) if has_sink else call(*args)


def _dense_body(q_ref, k_ref, v_ref, o_ref, q4_ref, s_a, s_b, p_a, p_b, al_a, al_b, m_ref, acc_ref, *, tq, tk, nk, rb):
    lane = lax.broadcasted_iota(I32, (1, LANES), 1)
    lo = lane < HEAD_DIM
    zero = jnp.zeros((), BF16)
    for h in range(4):
        blk = q_ref[0, :, (h // 2) * LANES:(h // 2 + 1) * LANES]
        q4_ref[h * tq:(h + 1) * tq, :] = jnp.where(lo, blk, zero) if h % 2 == 0 else jnp.where(lo, zero, blk)
    m_ref[...] = jnp.full(m_ref.shape, NEG_INF, F32)
    acc_ref[...] = jnp.zeros(acc_ref.shape, F32)
    p_b[...] = jnp.zeros(p_b.shape, BF16)
    al_b[...] = jnp.ones(al_b.shape, F32)
    nct = tk // LANES

    def scores(j, s_ref):
        start = pl.multiple_of(jnp.minimum(j, nk - 1) * tk, tk)
        s_ref[...] = lax.dot_general(q4_ref[...], k_ref[0, pl.ds(start, tk), :], NT_DIMS, preferred_element_type=F32)

    def weighted_values(j, p_ref, al_ref):
        start = pl.multiple_of(jnp.maximum(j, 0) * tk, tk)
        pv = None
        for c in range(tk // MXU_DEPTH):
            part = jnp.dot(p_ref[:, c * MXU_DEPTH:(c + 1) * MXU_DEPTH],
                           v_ref[0, pl.ds(pl.multiple_of(start + c * MXU_DEPTH, MXU_DEPTH), MXU_DEPTH), :],
                           preferred_element_type=F32)
            pv = part if pv is None else pv + part
        acc_ref[...] = al_ref[...] * acc_ref[...] + pv

    def softmax_update(s_ref, p_ref, al_ref):
        for r in range(4 * tq // rb):
            rows = slice(r * rb, (r + 1) * rb)
            mt = functools.reduce(jnp.maximum, [s_ref[rows, c * LANES:(c + 1) * LANES] for c in range(nct)])
            m_prev = m_ref[rows, :]
            m_new = jnp.maximum(m_prev, jnp.max(mt, axis=1, keepdims=True))
            al_ref[rows, :] = jnp.exp2(m_prev - m_new)
            m_ref[rows, :] = m_new
        for r in range(4 * tq // rb):
            rows = slice(r * rb, (r + 1) * rb)
            m_new = m_ref[rows, :]
            for c in range(nct):
                cols = slice(c * LANES, (c + 1) * LANES)
                p_ref[rows, cols] = jnp.exp2(s_ref[rows, cols] - m_new).astype(BF16)

    scores(0, s_a)

    def pair(i, carry):
        j = 2 * i
        scores(j + 1, s_b)
        weighted_values(j - 1, p_b, al_b)
        softmax_update(s_a, p_a, al_a)
        scores(j + 2, s_a)
        weighted_values(j, p_a, al_a)
        softmax_update(s_b, p_b, al_b)
        return carry

    lax.fori_loop(0, nk // 2, pair, 0)
    weighted_values(nk - 1, p_b, al_b)
    acc = acc_ref[...]
    o = acc / jnp.where(lo, pltpu.roll(acc, HEAD_DIM, 1), acc)
    for hp in range(2):
        even = o[(2 * hp) * tq:(2 * hp + 1) * tq]
        odd = pltpu.roll(o[(2 * hp + 1) * tq:(2 * hp + 2) * tq], HEAD_DIM, 1)
        o_ref[0, :, hp * LANES:(hp + 1) * LANES] = jnp.where(lo, even, odd).astype(o_ref.dtype)


def _dense_attention(q, k, v, *, tq=128, tk=512, rb=32):
    bsz, seq, _ = q.shape
    tq = min(tq, seq)
    tk = min(tk, seq // 2)
    assert seq % (2 * tk) == 0 and seq % tq == 0
    ngroups = k.shape[2] // LANES
    return pl.pallas_call(
        functools.partial(_dense_body, tq=tq, tk=tk, nk=seq // tk, rb=min(rb, 4 * tq)),
        grid=(bsz, ngroups, seq // tq),
        in_specs=[
            pl.BlockSpec((1, tq, 2 * LANES), lambda b, g, i: (b, i, g)),
            pl.BlockSpec((1, seq, LANES), lambda b, g, i: (b, 0, g)),
            pl.BlockSpec((1, seq, LANES), lambda b, g, i: (b, 0, g)),
        ],
        out_specs=pl.BlockSpec((1, tq, 2 * LANES), lambda b, g, i: (b, i, g)),
        out_shape=jax.ShapeDtypeStruct(q.shape, BF16),
        scratch_shapes=(
            [pltpu.VMEM((4 * tq, LANES), BF16)]
            + [pltpu.VMEM((4 * tq, tk), F32)] * 2
            + [pltpu.VMEM((4 * tq, tk), BF16)] * 2
            + [pltpu.VMEM((4 * tq, LANES), F32)] * 4
        ),
        compiler_params=_params(("parallel", "parallel", "parallel"), 48),
        name="dense_attention",
    )(q, k, v)


def _outproj_body(*refs, n_in, tm, nt, tiled):
    o_refs = refs[:n_in]
    w_ref, x_ref, g_ref, wr_ref, x2_ref, hn_ref, aff_ref = refs[n_in:]
    y = None
    c0 = 0
    for o_ref in o_refs:
        wd = o_ref.shape[1]
        part = jnp.dot(o_ref[...], w_ref[c0:c0 + wd, :], preferred_element_type=F32)
        y = part if y is None else y + part
        c0 += wd
    x2 = _load_rows(x_ref, tm, nt, tiled) + y
    ms = jnp.mean(x2 * x2, axis=-1, keepdims=True)
    hn = x2 * lax.rsqrt(ms + EPS) * g_ref[...]
    _store_rows_tiled(x2_ref, x2, tm, nt)
    _store_rows_tiled(hn_ref, hn, tm, nt)
    wr = wr_ref[...]
    wr_hi = wr.astype(BF16)
    wr_lo = (wr - wr_hi.astype(F32)).astype(BF16)
    hn_hi = hn.astype(BF16)
    hn_lo = (hn - hn_hi.astype(F32)).astype(BF16)
    lt = (lax.dot_general(wr_hi, hn_hi, NT_DIMS, preferred_element_type=F32)
          + lax.dot_general(wr_lo, hn_hi, NT_DIMS, preferred_element_type=F32)
          + lax.dot_general(wr_hi, hn_lo, NT_DIMS, preferred_element_type=F32))
    ex = jnp.exp(lt - jnp.max(lt, axis=0, keepdims=True))
    aff_ref[0] = ex / jnp.sum(ex, axis=0, keepdims=True)


def _out_proj(o_parts, w, x, g, wr_t, *, bsz, seq, tiled, tm=512):
    d_model = w.shape[1]
    nt = d_model // LANES
    tokens = bsz * seq
    tm = min(tm, seq)
    nblk_seq = seq // tm
    nexp = wr_t.shape[0]
    const = lambda i: (0, 0)
    x_spec = (pl.BlockSpec((tm * nt, LANES), lambda i: (i, 0)) if tiled
              else pl.BlockSpec((tm, d_model), lambda i: (i, 0)))
    in_specs = [pl.BlockSpec((tm, o.shape[1]), lambda i: (i, 0)) for o in o_parts]
    in_specs += [pl.BlockSpec(w.shape, const), x_spec, pl.BlockSpec(g.shape, const), pl.BlockSpec(wr_t.shape, const)]
    tiled_spec = pl.BlockSpec((tm * nt, LANES), lambda i: (i, 0))
    return pl.pallas_call(
        functools.partial(_outproj_body, n_in=len(o_parts), tm=tm, nt=nt, tiled=tiled),
        grid=(tokens // tm,),
        in_specs=in_specs,
        out_specs=[tiled_spec, tiled_spec,
                   pl.BlockSpec((1, nexp, tm), lambda i: (i // nblk_seq, 0, i % nblk_seq))],
        out_shape=[jax.ShapeDtypeStruct((tokens * nt, LANES), F32),
                   jax.ShapeDtypeStruct((tokens * nt, LANES), F32),
                   jax.ShapeDtypeStruct((bsz, nexp, seq), F32)],
        compiler_params=_params(("parallel",), 48),
        name="out_proj_router",
    )(*o_parts, w, x, g, wr_t)


CUMSUM_BLOCK = 256
INDEX_CHUNK = 128


def _lane_cumsum(x, tri):
    outs = []
    carry = jnp.zeros((x.shape[0], 1), F32)
    for k in range(x.shape[1] // CUMSUM_BLOCK):
        blk = x[:, k * CUMSUM_BLOCK:(k + 1) * CUMSUM_BLOCK].astype(BF16)
        c = jnp.dot(blk, tri, preferred_element_type=F32) + carry
        outs.append(c)
        carry = c[:, CUMSUM_BLOCK - 1:CUMSUM_BLOCK]
    return jnp.concatenate(outs, axis=1)


def _route_body(aff_ref, idx_ref, cs_ref, *, seq, cap, nexp):
    e = pl.program_id(1)

    @pl.when(e == 0)
    def _():
        aff = aff_ref[0]
        bits = pltpu.bitcast(aff, I32)

        def search(_, c):
            lo, hi = c
            mid = lo + ((hi - lo + 1) >> 1)
            cnt = jnp.sum(jnp.where(bits >= mid, 1.0, 0.0), axis=1, keepdims=True)
            ok = cnt >= cap
            return jnp.where(ok, mid, lo), jnp.where(ok, hi, mid - 1)

        lo0 = jnp.zeros((nexp, 1), I32)
        hi0 = jnp.full((nexp, 1), 0x7F800000, I32)
        thr, _ = lax.fori_loop(0, 32, search, (lo0, hi0))
        gt = bits > thr
        eq = bits == thr
        r = lax.broadcasted_iota(I32, (CUMSUM_BLOCK, CUMSUM_BLOCK), 0)
        c = lax.broadcasted_iota(I32, (CUMSUM_BLOCK, CUMSUM_BLOCK), 1)
        tri = jnp.where(r <= c, 1.0, 0.0).astype(BF16)
        need = cap - jnp.sum(jnp.where(gt, 1.0, 0.0), axis=1, keepdims=True)
        eq_rank = _lane_cumsum(jnp.where(eq, 1.0, 0.0), tri)
        sel = gt | (eq & (eq_rank <= need))
        cs = _lane_cumsum(jnp.where(sel, 1.0, 0.0), tri)
        for ee in range(nexp):
            cs_ref[ee] = cs[ee:ee + 1, :]

    ones = jnp.ones((8, LANES), BF16)
    for cc in range(cap // INDEX_CHUNK):
        slot = (lax.broadcasted_iota(I32, (INDEX_CHUNK, LANES), 0) + cc * INDEX_CHUNK).astype(F32)

        def tile(k, part):
            row = cs_ref[e, :, pl.ds(pl.multiple_of(k * LANES, LANES), LANES)]
            return part + jnp.where(row <= slot, 1.0, 0.0)

        part = lax.fori_loop(0, seq // LANES, tile, jnp.zeros((INDEX_CHUNK, LANES), F32))
        cnt = lax.dot_general(ones, part.astype(BF16), NT_DIMS, preferred_element_type=F32)
        idx_ref[0, :, cc * INDEX_CHUNK:(cc + 1) * INDEX_CHUNK] = cnt[0:1].astype(I32)


def _route(aff_t, cap):
    bsz, nexp, seq = aff_t.shape
    return pl.pallas_call(
        functools.partial(_route_body, seq=seq, cap=cap, nexp=nexp),
        grid=(bsz, nexp),
        in_specs=[pl.BlockSpec((1, nexp, seq), lambda b, e: (b, 0, 0))],
        out_specs=pl.BlockSpec((1, 1, cap), lambda b, e: (b * nexp + e, 0, 0)),
        out_shape=jax.ShapeDtypeStruct((bsz * nexp, 1, cap), I32),
        scratch_shapes=[pltpu.VMEM((nexp, 1, seq), F32)],
        compiler_params=_params(("parallel", "arbitrary"), 48),
        name="route",
    )(aff_t)


ROW_UNROLL = 8


def _gather_body(idx_ref, h_hbm, x_ref, hbuf, xbuf, sem, *, cap, nt):
    b = pl.program_id(0)
    e = pl.program_id(1)

    @pl.when(e == 0)
    def _():
        cp = pltpu.make_async_copy(h_hbm.at[b], hbuf, sem)
        cp.start()
        cp.wait()

    def rows(i, carry):
        for u in range(ROW_UNROLL):
            c = i * ROW_UNROLL + u
            t = idx_ref[0, 0, c]
            xbuf[pl.ds(pl.multiple_of(c * nt, nt), nt), :] = hbuf[pl.ds(pl.multiple_of(t * nt, nt), nt), :]
        return carry

    lax.fori_loop(0, cap // ROW_UNROLL, rows, 0)
    for s in range(nt):
        x_ref[0, :, s * LANES:(s + 1) * LANES] = xbuf[pl.ds(s, cap, stride=nt), :].astype(x_ref.dtype)


def _gather(idx, h_tiled, *, bsz, nexp, seq, cap, nt):
    return pl.pallas_call(
        functools.partial(_gather_body, cap=cap, nt=nt),
        grid=(bsz, nexp),
        in_specs=[pl.BlockSpec((1, 1, cap), lambda b, e: (b * nexp + e, 0, 0), memory_space=pltpu.SMEM),
                  pl.BlockSpec(memory_space=pl.ANY)],
        out_specs=pl.BlockSpec((1, cap, nt * LANES), lambda b, e: (e, b, 0)),
        out_shape=jax.ShapeDtypeStruct((nexp, bsz * cap, nt * LANES), BF16),
        scratch_shapes=[pltpu.VMEM((seq * nt, LANES), F32), pltpu.VMEM((cap * nt, LANES), F32),
                        pltpu.SemaphoreType.DMA(())],
        compiler_params=_params(("arbitrary", "arbitrary"), 56),
        name="moe_gather",
    )(idx, h_tiled)


def _ffn_body(x_ref, wg_ref, wu_ref, wd_ref, y_ref, acc_ref, *, cap, nt):
    f = pl.program_id(2)
    x = x_ref[0]
    g = jnp.dot(x, wg_ref[0].astype(BF16), preferred_element_type=F32)
    u = jnp.dot(x, wu_ref[0].astype(BF16), preferred_element_type=F32)
    act = (g * (1.0 / (1.0 + jnp.exp(-g))) * u).astype(BF16)
    y = jnp.dot(act, wd_ref[0].astype(BF16), preferred_element_type=F32)

    @pl.when(f == 0)
    def _():
        acc_ref[...] = y

    @pl.when(f > 0)
    def _():
        acc_ref[...] += y

    @pl.when(f == pl.num_programs(2) - 1)
    def _():
        _store_rows_tiled(y_ref.at[0], acc_ref[...], cap, nt)


def _ffn(x, wg, wu, wd, *, bsz, cap, tf=512):
    nexp, d_model, d_exp = wg.shape
    nt = d_model // LANES
    tf = min(tf, d_exp)
    return pl.pallas_call(
        functools.partial(_ffn_body, cap=cap, nt=nt),
        grid=(nexp, bsz, d_exp // tf),
        in_specs=[pl.BlockSpec((1, cap, d_model), lambda e, b, f: (e, b, 0)),
                  pl.BlockSpec((1, d_model, tf), lambda e, b, f: (e, 0, f)),
                  pl.BlockSpec((1, d_model, tf), lambda e, b, f: (e, 0, f)),
                  pl.BlockSpec((1, tf, d_model), lambda e, b, f: (e, f, 0))],
        out_specs=pl.BlockSpec((1, cap * nt, LANES), lambda e, b, f: (b * nexp + e, 0, 0)),
        out_shape=jax.ShapeDtypeStruct((bsz * nexp, cap * nt, LANES), F32),
        scratch_shapes=[pltpu.VMEM((cap, d_model), F32)],
        compiler_params=_params(("parallel", "parallel", "arbitrary"), 56),
        name="moe_ffn",
    )(x, wg, wu, wd)


def _combine_body(idx_ref, aff_ref, x_hbm, y_ref, o_hbm, acc, sem, *, cap, nt, nexp):
    b = pl.program_id(0)
    e = pl.program_id(1)

    @pl.when(e == 0)
    def _():
        cp = pltpu.make_async_copy(x_hbm.at[b], acc, sem)
        cp.start()
        cp.wait()

    def rows(i, carry):
        dst, vals = [], []
        for u in range(ROW_UNROLL):
            c = i * ROW_UNROLL + u
            t = idx_ref[0, 0, c]
            gate = aff_ref[0, 0, t]
            d = pl.ds(pl.multiple_of(t * nt, nt), nt)
            dst.append(d)
            vals.append(acc[d, :] + gate * y_ref[0, pl.ds(pl.multiple_of(c * nt, nt), nt), :])
        for d, val in zip(dst, vals):
            acc[d, :] = val
        return carry

    lax.fori_loop(0, cap // ROW_UNROLL, rows, 0)

    @pl.when(e == nexp - 1)
    def _():
        cp = pltpu.make_async_copy(acc, o_hbm.at[b], sem)
        cp.start()
        cp.wait()


def _combine(idx, aff_rows, x_tiled, y_tiled, *, bsz, nexp, seq, cap, nt):
    return pl.pallas_call(
        functools.partial(_combine_body, cap=cap, nt=nt, nexp=nexp),
        grid=(bsz, nexp),
        in_specs=[pl.BlockSpec((1, 1, cap), lambda b, e: (b * nexp + e, 0, 0), memory_space=pltpu.SMEM),
                  pl.BlockSpec((1, 1, seq), lambda b, e: (b * nexp + e, 0, 0), memory_space=pltpu.SMEM),
                  pl.BlockSpec(memory_space=pl.ANY),
                  pl.BlockSpec((1, cap * nt, LANES), lambda b, e: (b * nexp + e, 0, 0))],
        out_specs=pl.BlockSpec(memory_space=pl.ANY),
        out_shape=jax.ShapeDtypeStruct(x_tiled.shape, F32),
        scratch_shapes=[pltpu.VMEM((seq * nt, LANES), F32), pltpu.SemaphoreType.DMA(())],
        compiler_params=_params(("arbitrary", "arbitrary"), 56),
        name="moe_combine",
    )(idx, aff_rows, x_tiled, y_tiled)


def _moe(x2_t, hn_t, aff_t, wg, wu, wd, *, bsz, seq):
    nexp = aff_t.shape[1]
    nt = wg.shape[1] // LANES
    cap = CAPACITY_FACTOR * seq // nexp
    idx = _route(aff_t, cap)
    xg = _gather(idx, hn_t.reshape(bsz, seq * nt, LANES), bsz=bsz, nexp=nexp, seq=seq, cap=cap, nt=nt)
    y = _ffn(xg, wg, wu, wd, bsz=bsz, cap=cap)
    out = _combine(idx, aff_t.reshape(bsz * nexp, 1, seq), x2_t.reshape(bsz, seq * nt, LANES), y,
                   bsz=bsz, nexp=nexp, seq=seq, cap=cap, nt=nt)
    return out.reshape(bsz * seq * nt, LANES)


def _t5_bucket(rel):
    half = NUM_BUCKETS // 2
    ret = jnp.where(rel > 0, half, 0)
    n = jnp.abs(rel)
    nf = jnp.maximum(n, 1).astype(F32)
    large = MAX_EXACT + (jnp.log(nf / MAX_EXACT) / math.log(MAX_DISTANCE / MAX_EXACT)
                         * (half - MAX_EXACT)).astype(I32)
    large = jnp.minimum(large, half - 1)
    return ret + jnp.where(n < MAX_EXACT, n, large)


def _band_bias(rel_bias, nheads, qr, halo, radius, dil):
    rel = jnp.arange(qr + 2 * halo)[None, :] - halo - jnp.arange(qr)[:, None]
    onehot = (_t5_bucket(rel * dil)[None] == jnp.arange(NUM_BUCKETS)[:, None, None]).astype(F32)
    tile = jnp.sum(rel_bias[:nheads].astype(F32)[:, :, None, None] * onehot[None], axis=1)
    tile = jnp.where((jnp.abs(rel) <= radius)[None], tile, NEG_INF)
    return tile.reshape(nheads // 2, 2 * qr, qr + 2 * halo)


def _tile_gain(g, scale=1.0):
    return jnp.tile(g.astype(F32) * scale, LANES // HEAD_DIM)[None, :]


def _dup_heads(w, nheads):
    d = w.shape[0]
    w4 = w.reshape(d, nheads, 1, HEAD_DIM)
    return jnp.broadcast_to(w4, (d, nheads, 2, HEAD_DIM)).reshape(d, nheads * 2 * HEAD_DIM)


def _rope_tables(seq):
    half = HEAD_DIM // 2
    inv = ROPE_THETA ** (-jnp.arange(0, half, 2, dtype=F32) / half)
    t = jnp.arange(seq)
    row = (t // GRID_W).astype(F32)[:, None] * inv
    colp = (t % GRID_W).astype(F32)[:, None] * inv
    cos64 = jnp.concatenate([jnp.cos(row), jnp.cos(row), jnp.cos(colp), jnp.cos(colp)], axis=1)
    sin64 = jnp.concatenate([-jnp.sin(row), jnp.sin(row), -jnp.sin(colp), jnp.sin(colp)], axis=1)
    return jnp.tile(cos64, (1, 2)), jnp.tile(sin64, (1, 2))


def _block_diag_ones():
    r = jnp.arange(LANES)
    return (r[:, None] // HEAD_DIM == r[None, :] // HEAD_DIM).astype(BF16)


def kernel(x, rel_bias, l0_norm_attn, l0_w_in, l0_a_qnorm, l0_a_knorm, l0_b_qnorm, l0_b_knorm, l0_w_out,
           l0_norm_ffn, l0_router, l0_w_gate, l0_w_up, l0_w_down, l1_norm_attn, l1_w_in, l1_c_qnorm, l1_c_knorm,
           l1_sink, l1_w_out, l1_norm_ffn, l1_router, l1_w_gate, l1_w_up, l1_w_down):
    bsz, seq, d_model = x.shape
    nt = d_model // LANES
    scale = 1.0 / math.sqrt(HEAD_DIM)
    a_w = A_HEADS * HEAD_DIM
    bq_w = B_HEADS * HEAD_DIM
    bkv_w = B_KV_HEADS * HEAD_DIM
    cq_w = C_HEADS * HEAD_DIM
    ckv_w = C_KV_HEADS * HEAD_DIM
    bd = _block_diag_ones()
    cos_t, sin_t = _rope_tables(seq)

    c = 3 * a_w + bq_w
    w0 = jnp.concatenate([l0_w_in[:, :c], _dup_heads(l0_w_in[:, c:c + bkv_w], B_KV_HEADS),
                          _dup_heads(l0_w_in[:, c + bkv_w:], B_KV_HEADS)], axis=1).astype(BF16)
    gains0 = jnp.concatenate([_tile_gain(l0_a_qnorm, scale), _tile_gain(l0_a_knorm),
                              _tile_gain(l0_b_qnorm, scale * math.log2(math.e)), _tile_gain(l0_b_knorm)], axis=0)
    groups0 = ((0, a_w, "norm", 0), (a_w, a_w, "norm", 1), (2 * a_w, a_w, "plain", 0),
               (3 * a_w, bq_w, "rope", 2), (c, 2 * bkv_w, "rope", 3), (c + 2 * bkv_w, 2 * bkv_w, "value_ones", 0))
    qa, ka, va, qb, kb, vb = _norm_proj(x.reshape(bsz * seq, d_model), l0_norm_attn[None, :], w0, bd, gains0,
                                        cos_t, sin_t, groups0, seq=seq, tiled=False)
    state = None
    for n, (window, dil) in enumerate(A_BRANCHES):
        radius = window // (2 * dil)
        view = lambda t: t.reshape(bsz, seq // dil, dil * t.shape[-1])
        bias = _band_bias(rel_bias, A_HEADS, 2 * radius, radius, radius, dil)
        last = n == len(A_BRANCHES) - 1
        res = _band_attention(view(qa), view(ka), view(va), bias, kv_div=1, bias_mod=A_HEADS // 2,
                              qr=2 * radius, halo=radius, state=None if state is None else tuple(view(t) for t in state),
                              final=last)
        if last:
            oa = res.reshape(bsz * seq, a_w)
        else:
            state = tuple(t.reshape(bsz * seq, a_w) for t in res)
    ob = _dense_attention(qb.reshape(bsz, seq, bq_w), kb.reshape(bsz, seq, 2 * bkv_w),
                          vb.reshape(bsz, seq, 2 * bkv_w)).reshape(bsz * seq, bq_w)
    x2, hn, aff = _out_proj([oa, ob], l0_w_out.astype(BF16), x.reshape(bsz * seq, d_model), l0_norm_ffn[None, :],
                            l0_router.T, bsz=bsz, seq=seq, tiled=False)
    x3 = _moe(x2, hn, aff, l0_w_gate, l0_w_up, l0_w_down, bsz=bsz, seq=seq)

    w1 = jnp.concatenate([l1_w_in[:, :cq_w], _dup_heads(l1_w_in[:, cq_w:cq_w + ckv_w], C_KV_HEADS),
                          _dup_heads(l1_w_in[:, cq_w + ckv_w:], C_KV_HEADS)], axis=1).astype(BF16)
    gains1 = jnp.concatenate([_tile_gain(l1_c_qnorm, scale), _tile_gain(l1_c_knorm)], axis=0)
    groups1 = ((0, cq_w, "norm", 0), (cq_w, 2 * ckv_w, "norm", 1), (cq_w + 2 * ckv_w, 2 * ckv_w, "plain", 0))
    qc, kc, vc = _norm_proj(x3, l1_norm_attn[None, :], w1, bd, gains1, cos_t, sin_t, groups1, seq=seq, tiled=True)
    bias_c = _band_bias(rel_bias, C_HEADS, C_RADIUS, C_RADIUS, C_RADIUS, 1)
    shp = lambda t: t.reshape(bsz, seq, t.shape[-1])
    oc = _band_attention(shp(qc), shp(kc), shp(vc), bias_c, kv_div=C_HEADS // C_KV_HEADS // 2, bias_mod=C_HEADS // 2,
                         qr=C_RADIUS, halo=C_RADIUS, sink=l1_sink.astype(F32)).reshape(bsz * seq, cq_w)
    x4, hn1, aff1 = _out_proj([oc], l1_w_out.astype(BF16), x3, l1_norm_ffn[None, :], l1_router.T,
                              bsz=bsz, seq=seq, tiled=True)
    x5 = _moe(x4, hn1, aff1, l1_w_gate, l1_w_up, l1_w_down, bsz=bsz, seq=seq)
    return x5.reshape(bsz, seq, nt, LANES).reshape(bsz, seq, d_model)
```

```python
import functools
import math

import jax
import jax.numpy as jnp
from jax import lax
from jax.experimental import pallas as pl
from jax.experimental.pallas import tpu as pltpu

F32 = jnp.float32
BF16 = jnp.bfloat16
I32 = jnp.int32

LANES = 128
MXU_DEPTH = 256
HEAD_DIM = 64
NEG_INF = -1e30
EPS = 1e-6

A_HEADS = 8
A_BRANCHES = ((128, 1), (512, 4), (2048, 16))
B_HEADS = 8
B_KV_HEADS = 2
GRID_W = 64
ROPE_THETA = 10000.0
C_HEADS = 16
C_KV_HEADS = 4
C_RADIUS = 128
NUM_BUCKETS = 32
MAX_EXACT = 8
MAX_DISTANCE = 1024
N_EXPERTS = 16
CAPACITY_FACTOR = 2

MIB = 1024 * 1024
NT_DIMS = (((1,), (1,)), ((), ()))


def _params(semantics, vmem_mib):
    return pltpu.CompilerParams(dimension_semantics=semantics, vmem_limit_bytes=vmem_mib * MIB)


def _load_rows(x_ref, tm, nt, tiled):
    if not tiled:
        return x_ref[...]
    return jnp.concatenate([x_ref[pl.ds(s, tm, stride=nt), :] for s in range(nt)], axis=1)


def _store_rows_tiled(o_ref, val, tm, nt):
    for s in range(nt):
        o_ref[pl.ds(s, tm, stride=nt), :] = val[:, s * LANES:(s + 1) * LANES]


def _proj_body(x_ref, g_ref, w_ref, bd_ref, gains_ref, cos_ref, sin_ref, *out_refs, groups, tm, nt, tiled):
    x = _load_rows(x_ref, tm, nt, tiled)
    ms = jnp.mean(x * x, axis=-1, keepdims=True)
    xn = (x * lax.rsqrt(ms + EPS) * g_ref[...]).astype(BF16)
    lane = lax.broadcasted_iota(I32, (1, LANES), 1)
    first16 = (lane % 32) < 16
    bd = bd_ref[...]
    for (c0, width, kind, grow, _), o_ref in zip(groups, out_refs):
        y = jnp.dot(xn, w_ref[:, c0:c0 + width], preferred_element_type=F32)
        if kind == "plain":
            o_ref[...] = y.astype(o_ref.dtype)
            continue
        if kind == "value_ones":
            for j in range(width // LANES):
                o_ref[:, j * LANES:(j + 1) * LANES] = jnp.where(
                    lane < HEAD_DIM, y[:, j * LANES:(j + 1) * LANES], 1.0).astype(o_ref.dtype)
            continue
        gain = gains_ref[grow:grow + 1, :]
        for j in range(width // LANES):
            yj = y[:, j * LANES:(j + 1) * LANES]
            sq = yj * yj
            hi = sq.astype(BF16)
            lo = (sq - hi.astype(F32)).astype(BF16)
            ss = jnp.dot(hi, bd, preferred_element_type=F32) + jnp.dot(lo, bd, preferred_element_type=F32)
            yn = yj * lax.rsqrt(ss * (1.0 / HEAD_DIM) + EPS) * gain
            if kind == "rope":
                partner = jnp.where(first16, pltpu.roll(yn, LANES - 16, 1), pltpu.roll(yn, 16, 1))
                yn = yn * cos_ref[...] + partner * sin_ref[...]
            o_ref[:, j * LANES:(j + 1) * LANES] = yn.astype(o_ref.dtype)


def _norm_proj(x, g, w, bd, gains, cos_t, sin_t, groups, *, seq, tiled, tm=512):
    d_model = w.shape[0]
    nt = d_model // LANES
    tokens = x.shape[0] // nt if tiled else x.shape[0]
    tm = min(tm, seq)
    nblk_seq = seq // tm
    x_spec = (pl.BlockSpec((tm * nt, LANES), lambda i: (i, 0)) if tiled
              else pl.BlockSpec((tm, d_model), lambda i: (i, 0)))
    const = lambda i: (0, 0)
    in_specs = [
        x_spec,
        pl.BlockSpec(g.shape, const),
        pl.BlockSpec(w.shape, const),
        pl.BlockSpec(bd.shape, const),
        pl.BlockSpec(gains.shape, const),
        pl.BlockSpec((tm, LANES), lambda i: (i % nblk_seq, 0)),
        pl.BlockSpec((tm, LANES), lambda i: (i % nblk_seq, 0)),
    ]
    out_shape = [jax.ShapeDtypeStruct((tokens, wd), dt) for (_, wd, _, _, dt) in groups]
    out_specs = [pl.BlockSpec((tm, wd), lambda i: (i, 0)) for (_, wd, _, _, _) in groups]
    return pl.pallas_call(
        functools.partial(_proj_body, groups=groups, tm=tm, nt=nt, tiled=tiled),
        grid=(tokens // tm,),
        in_specs=in_specs,
        out_specs=out_specs,
        out_shape=out_shape,
        compiler_params=_params(("parallel",), 48),
        name="norm_proj",
    )(x, g, w, bd, gains, cos_t, sin_t)


def _band_body(*refs, tq, qr, halo, length, has_state, final, has_sink):
    it = iter(refs)
    sink_ref = next(it) if has_sink else None
    q_ref, kp_ref, km_ref, kn_ref, vp_ref, vm_ref, vn_ref, bias_ref = (next(it) for _ in range(8))
    if has_state:
        m_in, l_in, a_in = next(it), next(it), next(it)
    if final:
        o_ref = next(it)
    else:
        m_out, l_out, a_out = next(it), next(it), next(it)

    cb = pl.program_id(1)
    blk = pl.program_id(2)
    win = qr + 2 * halo
    kext = jnp.concatenate([kp_ref[0], km_ref[0], kn_ref[0]], axis=0)
    vext = jnp.concatenate([vp_ref[0], vm_ref[0], vn_ref[0]], axis=0)
    lane = lax.broadcasted_iota(I32, (1, LANES), 1)
    lo = lane < HEAD_DIM
    col = lax.broadcasted_iota(I32, (1, win), 1)
    head_row = lax.broadcasted_iota(I32, (2 * qr, 1), 0) < qr
    bias = bias_ref[0]
    zero = jnp.zeros((), BF16)

    def split(tile):
        return jnp.concatenate([tile[:, 0:1], tile[:, HEAD_DIM:HEAD_DIM + 1]], axis=0)

    def merge(colv):
        return jnp.where(lo, colv[:qr], colv[qr:])

    for s in range(tq // qr):
        rows = slice(s * qr, (s + 1) * qr)
        qs = q_ref[0, rows, :]
        q2 = jnp.concatenate([jnp.where(lo, qs, zero), jnp.where(lo, zero, qs)], axis=0)
        kw = kext[s * qr:s * qr + win]
        vw = vext[s * qr:s * qr + win]
        sc = lax.dot_general(q2, kw, NT_DIMS, preferred_element_type=F32) + bias
        kpos = blk * tq + (s * qr - halo) + col
        sc = jnp.where((kpos >= 0) & (kpos < length), sc, NEG_INF)
        m_cur = jnp.max(sc, axis=1, keepdims=True)
        if has_state:
            m_prev = split(m_in[0, rows, :])
            l_prev = split(l_in[0, rows, :])
            m_new = jnp.maximum(m_prev, m_cur)
        else:
            m_new = m_cur
        if has_sink:
            sk = jnp.where(head_row, sink_ref[2 * cb], sink_ref[2 * cb + 1])
            m_new = jnp.maximum(m_new, sk)
        p = jnp.exp(sc - m_new)
        l_new = jnp.sum(p, axis=1, keepdims=True)
        pv = jnp.dot(p.astype(BF16), vw, preferred_element_type=F32)
        acc = merge(pv)
        if has_state:
            alpha = jnp.exp(m_prev - m_new)
            l_new = l_new + alpha * l_prev
            acc = acc + merge(alpha) * a_in[0, rows, :]
        if has_sink:
            l_new = l_new + jnp.exp(sk - m_new)
        if final:
            o_ref[0, rows, :] = (acc / merge(l_new)).astype(o_ref.dtype)
        else:
            m_out[0, rows, :] = jnp.broadcast_to(merge(m_new), (qr, LANES))
            l_out[0, rows, :] = jnp.broadcast_to(merge(l_new), (qr, LANES))
            a_out[0, rows, :] = acc


def _band_attention(q, k, v, bias, *, kv_div, bias_mod, qr, halo, state=None, final=True, sink=None, tq=512):
    bsz, length, qcols = q.shape
    ncb = qcols // LANES
    tq = min(tq, length)
    nq = length // tq
    per = tq // halo
    nh = length // halo
    win = qr + 2 * halo

    def qmap(b, cb, i, *_):
        return (b, i, cb)

    def kmain(b, cb, i, *_):
        return (b, i, cb // kv_div)

    def kprev(b, cb, i, *_):
        return (b, jnp.maximum(i * per - 1, 0), cb // kv_div)

    def knext(b, cb, i, *_):
        return (b, jnp.minimum((i + 1) * per, nh - 1), cb // kv_div)

    def bmap(b, cb, i, *_):
        return (cb % bias_mod, 0, 0)

    main = lambda m: pl.BlockSpec((1, tq, LANES), m)
    edge = lambda m: pl.BlockSpec((1, halo, LANES), m)
    in_specs = [main(qmap), edge(kprev), main(kmain), edge(knext), edge(kprev), main(kmain), edge(knext),
                pl.BlockSpec((1, 2 * qr, win), bmap)]
    args = [q, k, k, k, v, v, v, bias]
    if state is not None:
        in_specs += [main(qmap)] * 3
        args += list(state)
    if final:
        out_shape = jax.ShapeDtypeStruct(q.shape, BF16)
        out_specs = main(qmap)
    else:
        out_shape = [jax.ShapeDtypeStruct(q.shape, F32)] * 3
        out_specs = [main(qmap)] * 3
    has_sink = sink is not None
    grid_spec = pltpu.PrefetchScalarGridSpec(
        num_scalar_prefetch=1 if has_sink else 0,
        grid=(bsz, ncb, nq),
        in_specs=in_specs,
        out_specs=out_specs,
    )
    body = functools.partial(_band_body, tq=tq, qr=qr, halo=halo, length=length,
                             has_state=state is not None, final=final, has_sink=has_sink)
    call = pl.pallas_call(body, grid_spec=grid_spec, out_shape=out_shape,
                          compiler_params=_params(("parallel", "parallel", "parallel"), 48),
                          name="band_attention")
    return call(sink, *args) if has_sink else call(*args)


DIL_QR = 128


def _dilated_body(q_ref, kp_ref, km_ref, kn_ref, vp_ref, vm_ref, vn_ref, *rest, branches, tq, halo, seq):
    nb = len(branches)
    bias_refs = rest[:nb]
    o_ref = rest[nb]
    kext, vext, s_a, s_b, p_a, p_b = rest[nb + 1:nb + 7]
    stats = rest[nb + 7:]
    blk = pl.program_id(2)
    qr = DIL_QR
    kext[0:halo, :] = kp_ref[0]
    kext[halo:halo + tq, :] = km_ref[0]
    kext[halo + tq:halo + tq + halo, :] = kn_ref[0]
    vext[0:halo, :] = vp_ref[0]
    vext[halo:halo + tq, :] = vm_ref[0]
    vext[halo + tq:halo + tq + halo, :] = vn_ref[0]
    p_b[...] = jnp.zeros(p_b.shape, BF16)
    lane = lax.broadcasted_iota(I32, (1, LANES), 1)
    lo = lane < HEAD_DIM
    zero = jnp.zeros((), BF16)
    nunits = tq // qr

    for bi, (dil, radius) in enumerate(branches):
        win = qr + 2 * radius
        m_st, l_st, a_st = stats[3 * bi:3 * bi + 3]
        bias_ref = bias_refs[bi]
        upr = nunits // dil
        off = halo - radius * dil
        col = lax.broadcasted_iota(I32, (1, win), 1)

        def first_row(u, dil=dil, upr=upr):
            u = jnp.clip(u, 0, nunits - 1)
            start = u // upr + (u % upr) * (qr * dil)
            return pl.multiple_of(start, qr) if dil == 1 else start

        def rows(start, n, dil=dil):
            return pl.ds(start, n) if dil == 1 else pl.ds(start, n, stride=dil)

        def scores(u, s_ref, first_row=first_row, rows=rows, off=off, win=win):
            start = first_row(u)
            qs = q_ref[0, rows(start, qr), :].astype(BF16)
            q2 = jnp.concatenate([jnp.where(lo, qs, zero), jnp.where(lo, zero, qs)], axis=0)
            kw = kext[rows(off + start, win), :].astype(BF16)
            s_ref[:, 0:win] = lax.dot_general(q2, kw, NT_DIMS, preferred_element_type=F32)

        def softmax(u, s_ref, p_ref, first_row=first_row, rows=rows, win=win, dil=dil, radius=radius,
                    bias_ref=bias_ref, m_st=m_st, l_st=l_st, col=col):
            start = first_row(u)
            sc = s_ref[:, 0:win] + bias_ref[0]
            kpos = blk * tq + start - radius * dil + dil * col
            sc = jnp.where((kpos >= 0) & (kpos < seq), sc, NEG_INF)
            m = jnp.max(sc, axis=1, keepdims=True)
            p = jnp.exp2(sc - m)
            l = jnp.sum(p, axis=1, keepdims=True)
            p_ref[:, 0:win] = p.astype(BF16)
            m_st[rows(start, qr), :] = jnp.where(lo, m[:qr], m[qr:])
            l_st[rows(start, qr), :] = jnp.where(lo, l[:qr], l[qr:])

        def values(u, p_ref, first_row=first_row, rows=rows, off=off, win=win, a_st=a_st):
            start = first_row(u)
            vw = vext[rows(off + start, win), :].astype(BF16)
            pv = jnp.dot(p_ref[:, 0:win], vw, preferred_element_type=F32)
            a_st[rows(start, qr), :] = jnp.where(lo, pv[:qr], pv[qr:])

        scores(0, s_a)

        def pair(i, carry, scores=scores, softmax=softmax, values=values):
            u = 2 * i
            scores(u + 1, s_b)
            values(u - 1, p_b)
            softmax(u, s_a, p_a)
            scores(u + 2, s_a)
            values(u, p_a)
            softmax(u + 1, s_b, p_b)
            return carry

        lax.fori_loop(0, nunits // 2, pair, 0)
        values(nunits - 1, p_b)

    chunk = 256
    for c in range(tq // chunk):
        rws = slice(c * chunk, (c + 1) * chunk)
        ms = [stats[3 * bi][rws, :] for bi in range(nb)]
        mx = functools.reduce(jnp.maximum, ms)
        num = None
        den = None
        for bi in range(nb):
            w = jnp.exp2(ms[bi] - mx)
            n_b = w * stats[3 * bi + 2][rws, :]
            d_b = w * stats[3 * bi + 1][rws, :]
            num = n_b if num is None else num + n_b
            den = d_b if den is None else den + d_b
        o_ref[0, rws, :] = (num / den).astype(o_ref.dtype)


def _dilated_attention(q, k, v, biases, branches, *, tq=2048):
    bsz, seq, cols = q.shape
    ncb = cols // LANES
    halo = max(d * r for d, r in branches)
    tq = min(tq, seq)
    assert tq % halo == 0 and seq % tq == 0 and (tq // DIL_QR) % 2 == 0
    assert all((tq // DIL_QR) % d == 0 for d, _ in branches)
    per = tq // halo
    nh = seq // halo
    maxwin = max(DIL_QR + 2 * r for _, r in branches)

    main = pl.BlockSpec((1, tq, LANES), lambda b, cb, i: (b, i, cb))
    prev = pl.BlockSpec((1, halo, LANES), lambda b, cb, i: (b, jnp.maximum(i * per - 1, 0), cb))
    nxt = pl.BlockSpec((1, halo, LANES), lambda b, cb, i: (b, jnp.minimum((i + 1) * per, nh - 1), cb))
    bias_specs = [pl.BlockSpec((1,) + bz.shape[1:], lambda b, cb, i: (cb, 0, 0)) for bz in biases]
    ext = tq + 2 * halo
    scratch = ([pltpu.VMEM((ext, LANES), F32)] * 2
               + [pltpu.VMEM((2 * DIL_QR, maxwin), F32)] * 2
               + [pltpu.VMEM((2 * DIL_QR, maxwin), BF16)] * 2
               + [pltpu.VMEM((tq, LANES), F32)] * (3 * len(branches)))
    return pl.pallas_call(
        functools.partial(_dilated_body, branches=tuple(branches), tq=tq, halo=halo, seq=seq),
        grid=(bsz, ncb, seq // tq),
        in_specs=[main, prev, main, nxt, prev, main, nxt] + bias_specs,
        out_specs=main,
        out_shape=jax.ShapeDtypeStruct(q.shape, BF16),
        scratch_shapes=scratch,
        compiler_params=_params(("parallel", "parallel", "parallel"), 48),
        name="dilated_attention",
    )(q, k, k, k, v, v, v, *biases)


def _dense_body(q_ref, k_ref, v_ref, o_ref, q4_ref, s_a, s_b, p_a, p_b, al_a, al_b, m_ref, acc_ref, *, tq, tk, nk, rb):
    lane = lax.broadcasted_iota(I32, (1, LANES), 1)
    lo = lane < HEAD_DIM
    zero = jnp.zeros((), BF16)
    for h in range(4):
        blk = q_ref[0, :, (h // 2) * LANES:(h // 2 + 1) * LANES]
        q4_ref[h * tq:(h + 1) * tq, :] = jnp.where(lo, blk, zero) if h % 2 == 0 else jnp.where(lo, zero, blk)
    m_ref[...] = jnp.full(m_ref.shape, NEG_INF, F32)
    acc_ref[...] = jnp.zeros(acc_ref.shape, F32)
    p_b[...] = jnp.zeros(p_b.shape, BF16)
    al_b[...] = jnp.ones(al_b.shape, F32)
    nct = tk // LANES

    def scores(j, s_ref):
        start = pl.multiple_of(jnp.minimum(j, nk - 1) * tk, tk)
        s_ref[...] = lax.dot_general(q4_ref[...], k_ref[0, pl.ds(start, tk), :], NT_DIMS, preferred_element_type=F32)

    def weighted_values(j, p_ref, al_ref):
        start = pl.multiple_of(jnp.maximum(j, 0) * tk, tk)
        pv = None
        for c in range(tk // MXU_DEPTH):
            part = jnp.dot(p_ref[:, c * MXU_DEPTH:(c + 1) * MXU_DEPTH],
                           v_ref[0, pl.ds(pl.multiple_of(start + c * MXU_DEPTH, MXU_DEPTH), MXU_DEPTH), :],
                           preferred_element_type=F32)
            pv = part if pv is None else pv + part
        acc_ref[...] = al_ref[...] * acc_ref[...] + pv

    def softmax_update(s_ref, p_ref, al_ref):
        for r in range(4 * tq // rb):
            rows = slice(r * rb, (r + 1) * rb)
            mt = functools.reduce(jnp.maximum, [s_ref[rows, c * LANES:(c + 1) * LANES] for c in range(nct)])
            m_prev = m_ref[rows, :]
            m_new = jnp.maximum(m_prev, jnp.max(mt, axis=1, keepdims=True))
            al_ref[rows, :] = jnp.exp2(m_prev - m_new)
            m_ref[rows, :] = m_new
        for r in range(4 * tq // rb):
            rows = slice(r * rb, (r + 1) * rb)
            m_new = m_ref[rows, :]
            for c in range(nct):
                cols = slice(c * LANES, (c + 1) * LANES)
                p_ref[rows, cols] = jnp.exp2(s_ref[rows, cols] - m_new).astype(BF16)

    scores(0, s_a)

    def pair(i, carry):
        j = 2 * i
        scores(j + 1, s_b)
        weighted_values(j - 1, p_b, al_b)
        softmax_update(s_a, p_a, al_a)
        scores(j + 2, s_a)
        weighted_values(j, p_a, al_a)
        softmax_update(s_b, p_b, al_b)
        return carry

    lax.fori_loop(0, nk // 2, pair, 0)
    weighted_values(nk - 1, p_b, al_b)
    acc = acc_ref[...]
    o = acc / jnp.where(lo, pltpu.roll(acc, HEAD_DIM, 1), acc)
    for hp in range(2):
        even = o[(2 * hp) * tq:(2 * hp + 1) * tq]
        odd = pltpu.roll(o[(2 * hp + 1) * tq:(2 * hp + 2) * tq], HEAD_DIM, 1)
        o_ref[0, :, hp * LANES:(hp + 1) * LANES] = jnp.where(lo, even, odd).astype(o_ref.dtype)


def _dense_attention(q, k, v, *, tq=128, tk=512, rb=32):
    bsz, seq, _ = q.shape
    tq = min(tq, seq)
    tk = min(tk, seq // 2)
    assert seq % (2 * tk) == 0 and seq % tq == 0
    ngroups = k.shape[2] // LANES
    return pl.pallas_call(
        functools.partial(_dense_body, tq=tq, tk=tk, nk=seq // tk, rb=min(rb, 4 * tq)),
        grid=(bsz, ngroups, seq // tq),
        in_specs=[
            pl.BlockSpec((1, tq, 2 * LANES), lambda b, g, i: (b, i, g)),
            pl.BlockSpec((1, seq, LANES), lambda b, g, i: (b, 0, g)),
            pl.BlockSpec((1, seq, LANES), lambda b, g, i: (b, 0, g)),
        ],
        out_specs=pl.BlockSpec((1, tq, 2 * LANES), lambda b, g, i: (b, i, g)),
        out_shape=jax.ShapeDtypeStruct(q.shape, BF16),
        scratch_shapes=(
            [pltpu.VMEM((4 * tq, LANES), BF16)]
            + [pltpu.VMEM((4 * tq, tk), F32)] * 2
            + [pltpu.VMEM((4 * tq, tk), BF16)] * 2
            + [pltpu.VMEM((4 * tq, LANES), F32)] * 4
        ),
        compiler_params=_params(("parallel", "parallel", "parallel"), 48),
        name="dense_attention",
    )(q, k, v)


def _outproj_body(*refs, n_in, tm, nt, tiled):
    o_refs = refs[:n_in]
    w_ref, x_ref, g_ref, wr_ref, x2_ref, hn_ref, aff_ref = refs[n_in:]
    y = None
    c0 = 0
    for o_ref in o_refs:
        wd = o_ref.shape[1]
        part = jnp.dot(o_ref[...], w_ref[c0:c0 + wd, :], preferred_element_type=F32)
        y = part if y is None else y + part
        c0 += wd
    x2 = _load_rows(x_ref, tm, nt, tiled) + y
    ms = jnp.mean(x2 * x2, axis=-1, keepdims=True)
    hn = x2 * lax.rsqrt(ms + EPS) * g_ref[...]
    _store_rows_tiled(x2_ref, x2, tm, nt)
    _store_rows_tiled(hn_ref, hn, tm, nt)
    wr = wr_ref[...]
    wr_hi = wr.astype(BF16)
    wr_lo = (wr - wr_hi.astype(F32)).astype(BF16)
    hn_hi = hn.astype(BF16)
    hn_lo = (hn - hn_hi.astype(F32)).astype(BF16)
    lt = (lax.dot_general(wr_hi, hn_hi, NT_DIMS, preferred_element_type=F32)
          + lax.dot_general(wr_lo, hn_hi, NT_DIMS, preferred_element_type=F32)
          + lax.dot_general(wr_hi, hn_lo, NT_DIMS, preferred_element_type=F32))
    ex = jnp.exp(lt - jnp.max(lt, axis=0, keepdims=True))
    aff_ref[0] = ex / jnp.sum(ex, axis=0, keepdims=True)


def _out_proj(o_parts, w, x, g, wr_t, *, bsz, seq, tiled, tm=512):
    d_model = w.shape[1]
    nt = d_model // LANES
    tokens = bsz * seq
    tm = min(tm, seq)
    nblk_seq = seq // tm
    nexp = wr_t.shape[0]
    const = lambda i: (0, 0)
    x_spec = (pl.BlockSpec((tm * nt, LANES), lambda i: (i, 0)) if tiled
              else pl.BlockSpec((tm, d_model), lambda i: (i, 0)))
    in_specs = [pl.BlockSpec((tm, o.shape[1]), lambda i: (i, 0)) for o in o_parts]
    in_specs += [pl.BlockSpec(w.shape, const), x_spec, pl.BlockSpec(g.shape, const), pl.BlockSpec(wr_t.shape, const)]
    tiled_spec = pl.BlockSpec((tm * nt, LANES), lambda i: (i, 0))
    return pl.pallas_call(
        functools.partial(_outproj_body, n_in=len(o_parts), tm=tm, nt=nt, tiled=tiled),
        grid=(tokens // tm,),
        in_specs=in_specs,
        out_specs=[tiled_spec, tiled_spec,
                   pl.BlockSpec((1, nexp, tm), lambda i: (i // nblk_seq, 0, i % nblk_seq))],
        out_shape=[jax.ShapeDtypeStruct((tokens * nt, LANES), F32),
                   jax.ShapeDtypeStruct((tokens * nt, LANES), F32),
                   jax.ShapeDtypeStruct((bsz, nexp, seq), F32)],
        compiler_params=_params(("parallel",), 48),
        name="out_proj_router",
    )(*o_parts, w, x, g, wr_t)


CUMSUM_BLOCK = 256
INDEX_CHUNK = 128


def _lane_cumsum(x, tri):
    outs = []
    carry = jnp.zeros((x.shape[0], 1), F32)
    for k in range(x.shape[1] // CUMSUM_BLOCK):
        blk = x[:, k * CUMSUM_BLOCK:(k + 1) * CUMSUM_BLOCK].astype(BF16)
        c = jnp.dot(blk, tri, preferred_element_type=F32) + carry
        outs.append(c)
        carry = c[:, CUMSUM_BLOCK - 1:CUMSUM_BLOCK]
    return jnp.concatenate(outs, axis=1)


def _route_body(aff_ref, idx_ref, cs_ref, *, seq, cap, nexp):
    e = pl.program_id(1)

    @pl.when(e == 0)
    def _():
        aff = aff_ref[0]
        bits = pltpu.bitcast(aff, I32)

        def search(_, c):
            lo, hi = c
            mid = lo + ((hi - lo + 1) >> 1)
            cnt = jnp.sum(jnp.where(bits >= mid, 1.0, 0.0), axis=1, keepdims=True)
            ok = cnt >= cap
            return jnp.where(ok, mid, lo), jnp.where(ok, hi, mid - 1)

        lo0 = jnp.zeros((nexp, 1), I32)
        hi0 = jnp.full((nexp, 1), 0x7F800000, I32)
        thr, _ = lax.fori_loop(0, 32, search, (lo0, hi0))
        gt = bits > thr
        eq = bits == thr
        r = lax.broadcasted_iota(I32, (CUMSUM_BLOCK, CUMSUM_BLOCK), 0)
        c = lax.broadcasted_iota(I32, (CUMSUM_BLOCK, CUMSUM_BLOCK), 1)
        tri = jnp.where(r <= c, 1.0, 0.0).astype(BF16)
        need = cap - jnp.sum(jnp.where(gt, 1.0, 0.0), axis=1, keepdims=True)
        eq_rank = _lane_cumsum(jnp.where(eq, 1.0, 0.0), tri)
        sel = gt | (eq & (eq_rank <= need))
        cs = _lane_cumsum(jnp.where(sel, 1.0, 0.0), tri)
        for ee in range(nexp):
            cs_ref[ee] = cs[ee:ee + 1, :]

    ones = jnp.ones((8, LANES), BF16)
    for cc in range(cap // INDEX_CHUNK):
        slot = (lax.broadcasted_iota(I32, (INDEX_CHUNK, LANES), 0) + cc * INDEX_CHUNK).astype(F32)

        def tile(k, part):
            row = cs_ref[e, :, pl.ds(pl.multiple_of(k * LANES, LANES), LANES)]
            return part + jnp.where(row <= slot, 1.0, 0.0)

        part = lax.fori_loop(0, seq // LANES, tile, jnp.zeros((INDEX_CHUNK, LANES), F32))
        cnt = lax.dot_general(ones, part.astype(BF16), NT_DIMS, preferred_element_type=F32)
        idx_ref[0, :, cc * INDEX_CHUNK:(cc + 1) * INDEX_CHUNK] = cnt[0:1].astype(I32)


def _route(aff_t, cap):
    bsz, nexp, seq = aff_t.shape
    return pl.pallas_call(
        functools.partial(_route_body, seq=seq, cap=cap, nexp=nexp),
        grid=(bsz, nexp),
        in_specs=[pl.BlockSpec((1, nexp, seq), lambda b, e: (b, 0, 0))],
        out_specs=pl.BlockSpec((1, 1, cap), lambda b, e: (b * nexp + e, 0, 0)),
        out_shape=jax.ShapeDtypeStruct((bsz * nexp, 1, cap), I32),
        scratch_shapes=[pltpu.VMEM((nexp, 1, seq), F32)],
        compiler_params=_params(("parallel", "arbitrary"), 48),
        name="route",
    )(aff_t)


ROW_UNROLL = 8


def _gather_body(idx_ref, h_hbm, x_ref, hbuf, xbuf, sem, *, cap, nt):
    b = pl.program_id(0)
    e = pl.program_id(1)

    @pl.when(e == 0)
    def _():
        cp = pltpu.make_async_copy(h_hbm.at[b], hbuf, sem)
        cp.start()
        cp.wait()

    def rows(i, carry):
        for u in range(ROW_UNROLL):
            c = i * ROW_UNROLL + u
            t = idx_ref[0, 0, c]
            xbuf[pl.ds(pl.multiple_of(c * nt, nt), nt), :] = hbuf[pl.ds(pl.multiple_of(t * nt, nt), nt), :]
        return carry

    lax.fori_loop(0, cap // ROW_UNROLL, rows, 0)
    for s in range(nt):
        x_ref[0, :, s * LANES:(s + 1) * LANES] = xbuf[pl.ds(s, cap, stride=nt), :].astype(x_ref.dtype)


def _gather(idx, h_tiled, *, bsz, nexp, seq, cap, nt):
    return pl.pallas_call(
        functools.partial(_gather_body, cap=cap, nt=nt),
        grid=(bsz, nexp),
        in_specs=[pl.BlockSpec((1, 1, cap), lambda b, e: (b * nexp + e, 0, 0), memory_space=pltpu.SMEM),
                  pl.BlockSpec(memory_space=pl.ANY)],
        out_specs=pl.BlockSpec((1, cap, nt * LANES), lambda b, e: (e, b, 0)),
        out_shape=jax.ShapeDtypeStruct((nexp, bsz * cap, nt * LANES), BF16),
        scratch_shapes=[pltpu.VMEM((seq * nt, LANES), F32), pltpu.VMEM((cap * nt, LANES), F32),
                        pltpu.SemaphoreType.DMA(())],
        compiler_params=_params(("arbitrary", "arbitrary"), 56),
        name="moe_gather",
    )(idx, h_tiled)


def _ffn_body(x_ref, wg_ref, wu_ref, wd_ref, y_ref, acc_ref, *, cap, nt):
    f = pl.program_id(2)
    x = x_ref[0]
    g = jnp.dot(x, wg_ref[0].astype(BF16), preferred_element_type=F32)
    u = jnp.dot(x, wu_ref[0].astype(BF16), preferred_element_type=F32)
    act = (g * (1.0 / (1.0 + jnp.exp(-g))) * u).astype(BF16)
    y = jnp.dot(act, wd_ref[0].astype(BF16), preferred_element_type=F32)

    @pl.when(f == 0)
    def _():
        acc_ref[...] = y

    @pl.when(f > 0)
    def _():
        acc_ref[...] += y

    @pl.when(f == pl.num_programs(2) - 1)
    def _():
        _store_rows_tiled(y_ref.at[0], acc_ref[...], cap, nt)


def _ffn(x, wg, wu, wd, *, bsz, cap, tf=512):
    nexp, d_model, d_exp = wg.shape
    nt = d_model // LANES
    tf = min(tf, d_exp)
    return pl.pallas_call(
        functools.partial(_ffn_body, cap=cap, nt=nt),
        grid=(nexp, bsz, d_exp // tf),
        in_specs=[pl.BlockSpec((1, cap, d_model), lambda e, b, f: (e, b, 0)),
                  pl.BlockSpec((1, d_model, tf), lambda e, b, f: (e, 0, f)),
                  pl.BlockSpec((1, d_model, tf), lambda e, b, f: (e, 0, f)),
                  pl.BlockSpec((1, tf, d_model), lambda e, b, f: (e, f, 0))],
        out_specs=pl.BlockSpec((1, cap * nt, LANES), lambda e, b, f: (b * nexp + e, 0, 0)),
        out_shape=jax.ShapeDtypeStruct((bsz * nexp, cap * nt, LANES), F32),
        scratch_shapes=[pltpu.VMEM((cap, d_model), F32)],
        compiler_params=_params(("parallel", "parallel", "arbitrary"), 56),
        name="moe_ffn",
    )(x, wg, wu, wd)


def _combine_body(idx_ref, aff_ref, x_hbm, y_ref, o_hbm, acc, sem, *, cap, nt, nexp):
    b = pl.program_id(0)
    e = pl.program_id(1)

    @pl.when(e == 0)
    def _():
        cp = pltpu.make_async_copy(x_hbm.at[b], acc, sem)
        cp.start()
        cp.wait()

    def rows(i, carry):
        dst, vals = [], []
        for u in range(ROW_UNROLL):
            c = i * ROW_UNROLL + u
            t = idx_ref[0, 0, c]
            gate = aff_ref[0, 0, t]
            d = pl.ds(pl.multiple_of(t * nt, nt), nt)
            dst.append(d)
            vals.append(acc[d, :] + gate * y_ref[0, pl.ds(pl.multiple_of(c * nt, nt), nt), :])
        for d, val in zip(dst, vals):
            acc[d, :] = val
        return carry

    lax.fori_loop(0, cap // ROW_UNROLL, rows, 0)

    @pl.when(e == nexp - 1)
    def _():
        cp = pltpu.make_async_copy(acc, o_hbm.at[b], sem)
        cp.start()
        cp.wait()


def _combine(idx, aff_rows, x_tiled, y_tiled, *, bsz, nexp, seq, cap, nt):
    return pl.pallas_call(
        functools.partial(_combine_body, cap=cap, nt=nt, nexp=nexp),
        grid=(bsz, nexp),
        in_specs=[pl.BlockSpec((1, 1, cap), lambda b, e: (b * nexp + e, 0, 0), memory_space=pltpu.SMEM),
                  pl.BlockSpec((1, 1, seq), lambda b, e: (b * nexp + e, 0, 0), memory_space=pltpu.SMEM),
                  pl.BlockSpec(memory_space=pl.ANY),
                  pl.BlockSpec((1, cap * nt, LANES), lambda b, e: (b * nexp + e, 0, 0))],
        out_specs=pl.BlockSpec(memory_space=pl.ANY),
        out_shape=jax.ShapeDtypeStruct(x_tiled.shape, F32),
        scratch_shapes=[pltpu.VMEM((seq * nt, LANES), F32), pltpu.SemaphoreType.DMA(())],
        compiler_params=_params(("arbitrary", "arbitrary"), 56),
        name="moe_combine",
    )(idx, aff_rows, x_tiled, y_tiled)


def _moe(x2_t, hn_t, aff_t, wg, wu, wd, *, bsz, seq):
    nexp = aff_t.shape[1]
    nt = wg.shape[1] // LANES
    cap = CAPACITY_FACTOR * seq // nexp
    idx = _route(aff_t, cap)
    xg = _gather(idx, hn_t.reshape(bsz, seq * nt, LANES), bsz=bsz, nexp=nexp, seq=seq, cap=cap, nt=nt)
    y = _ffn(xg, wg, wu, wd, bsz=bsz, cap=cap)
    out = _combine(idx, aff_t.reshape(bsz * nexp, 1, seq), x2_t.reshape(bsz, seq * nt, LANES), y,
                   bsz=bsz, nexp=nexp, seq=seq, cap=cap, nt=nt)
    return out.reshape(bsz * seq * nt, LANES)


def _t5_bucket(rel):
    half = NUM_BUCKETS // 2
    ret = jnp.where(rel > 0, half, 0)
    n = jnp.abs(rel)
    nf = jnp.maximum(n, 1).astype(F32)
    large = MAX_EXACT + (jnp.log(nf / MAX_EXACT) / math.log(MAX_DISTANCE / MAX_EXACT)
                         * (half - MAX_EXACT)).astype(I32)
    large = jnp.minimum(large, half - 1)
    return ret + jnp.where(n < MAX_EXACT, n, large)


def _band_bias(rel_bias, nheads, qr, halo, radius, dil):
    rel = jnp.arange(qr + 2 * halo)[None, :] - halo - jnp.arange(qr)[:, None]
    onehot = (_t5_bucket(rel * dil)[None] == jnp.arange(NUM_BUCKETS)[:, None, None]).astype(F32)
    tile = jnp.sum(rel_bias[:nheads].astype(F32)[:, :, None, None] * onehot[None], axis=1)
    tile = jnp.where((jnp.abs(rel) <= radius)[None], tile, NEG_INF)
    return tile.reshape(nheads // 2, 2 * qr, qr + 2 * halo)


def _tile_gain(g, scale=1.0):
    return jnp.tile(g.astype(F32) * scale, LANES // HEAD_DIM)[None, :]


def _dup_heads(w, nheads):
    d = w.shape[0]
    w4 = w.reshape(d, nheads, 1, HEAD_DIM)
    return jnp.broadcast_to(w4, (d, nheads, 2, HEAD_DIM)).reshape(d, nheads * 2 * HEAD_DIM)


def _rope_tables(seq):
    half = HEAD_DIM // 2
    inv = ROPE_THETA ** (-jnp.arange(0, half, 2, dtype=F32) / half)
    t = jnp.arange(seq)
    row = (t // GRID_W).astype(F32)[:, None] * inv
    colp = (t % GRID_W).astype(F32)[:, None] * inv
    cos64 = jnp.concatenate([jnp.cos(row), jnp.cos(row), jnp.cos(colp), jnp.cos(colp)], axis=1)
    sin64 = jnp.concatenate([-jnp.sin(row), jnp.sin(row), -jnp.sin(colp), jnp.sin(colp)], axis=1)
    return jnp.tile(cos64, (1, 2)), jnp.tile(sin64, (1, 2))


def _block_diag_ones():
    r = jnp.arange(LANES)
    return (r[:, None] // HEAD_DIM == r[None, :] // HEAD_DIM).astype(BF16)


def kernel(x, rel_bias, l0_norm_attn, l0_w_in, l0_a_qnorm, l0_a_knorm, l0_b_qnorm, l0_b_knorm, l0_w_out,
           l0_norm_ffn, l0_router, l0_w_gate, l0_w_up, l0_w_down, l1_norm_attn, l1_w_in, l1_c_qnorm, l1_c_knorm,
           l1_sink, l1_w_out, l1_norm_ffn, l1_router, l1_w_gate, l1_w_up, l1_w_down):
    bsz, seq, d_model = x.shape
    nt = d_model // LANES
    scale = 1.0 / math.sqrt(HEAD_DIM)
    a_w = A_HEADS * HEAD_DIM
    bq_w = B_HEADS * HEAD_DIM
    bkv_w = B_KV_HEADS * HEAD_DIM
    cq_w = C_HEADS * HEAD_DIM
    ckv_w = C_KV_HEADS * HEAD_DIM
    bd = _block_diag_ones()
    cos_t, sin_t = _rope_tables(seq)

    c = 3 * a_w + bq_w
    w0 = jnp.concatenate([l0_w_in[:, :c], _dup_heads(l0_w_in[:, c:c + bkv_w], B_KV_HEADS),
                          _dup_heads(l0_w_in[:, c + bkv_w:], B_KV_HEADS)], axis=1).astype(BF16)
    log2e = math.log2(math.e)
    gains0 = jnp.concatenate([_tile_gain(l0_a_qnorm, scale * log2e), _tile_gain(l0_a_knorm),
                              _tile_gain(l0_b_qnorm, scale * log2e), _tile_gain(l0_b_knorm)], axis=0)
    groups0 = ((0, a_w, "norm", 0, F32), (a_w, a_w, "norm", 1, F32), (2 * a_w, a_w, "plain", 0, F32),
               (3 * a_w, bq_w, "rope", 2, BF16), (c, 2 * bkv_w, "rope", 3, BF16),
               (c + 2 * bkv_w, 2 * bkv_w, "value_ones", 0, BF16))
    qa, ka, va, qb, kb, vb = _norm_proj(x.reshape(bsz * seq, d_model), l0_norm_attn[None, :], w0, bd, gains0,
                                        cos_t, sin_t, groups0, seq=seq, tiled=False)
    branches = tuple((dil, window // (2 * dil)) for window, dil in A_BRANCHES)
    biases = [_band_bias(rel_bias, A_HEADS, DIL_QR, radius, radius, dil) * log2e for dil, radius in branches]
    shp = lambda t: t.reshape(bsz, seq, t.shape[-1])
    oa = _dilated_attention(shp(qa), shp(ka), shp(va), biases, branches).reshape(bsz * seq, a_w)
    ob = _dense_attention(qb.reshape(bsz, seq, bq_w), kb.reshape(bsz, seq, 2 * bkv_w),
                          vb.reshape(bsz, seq, 2 * bkv_w)).reshape(bsz * seq, bq_w)
    x2, hn, aff = _out_proj([oa, ob], l0_w_out.astype(BF16), x.reshape(bsz * seq, d_model), l0_norm_ffn[None, :],
                            l0_router.T, bsz=bsz, seq=seq, tiled=False)
    x3 = _moe(x2, hn, aff, l0_w_gate, l0_w_up, l0_w_down, bsz=bsz, seq=seq)

    w1 = jnp.concatenate([l1_w_in[:, :cq_w], _dup_heads(l1_w_in[:, cq_w:cq_w + ckv_w], C_KV_HEADS),
                          _dup_heads(l1_w_in[:, cq_w + ckv_w:], C_KV_HEADS)], axis=1).astype(BF16)
    gains1 = jnp.concatenate([_tile_gain(l1_c_qnorm, scale), _tile_gain(l1_c_knorm)], axis=0)
    groups1 = ((0, cq_w, "norm", 0, BF16), (cq_w, 2 * ckv_w, "norm", 1, BF16),
               (cq_w + 2 * ckv_w, 2 * ckv_w, "plain", 0, BF16))
    qc, kc, vc = _norm_proj(x3, l1_norm_attn[None, :], w1, bd, gains1, cos_t, sin_t, groups1, seq=seq, tiled=True)
    bias_c = _band_bias(rel_bias, C_HEADS, C_RADIUS, C_RADIUS, C_RADIUS, 1)
    oc = _band_attention(shp(qc), shp(kc), shp(vc), bias_c, kv_div=C_HEADS // C_KV_HEADS // 2, bias_mod=C_HEADS // 2,
                         qr=C_RADIUS, halo=C_RADIUS, sink=l1_sink.astype(F32)).reshape(bsz * seq, cq_w)
    x4, hn1, aff1 = _out_proj([oc], l1_w_out.astype(BF16), x3, l1_norm_ffn[None, :], l1_router.T,
                              bsz=bsz, seq=seq, tiled=True)
    x5 = _moe(x4, hn1, aff1, l1_w_gate, l1_w_up, l1_w_down, bsz=bsz, seq=seq)
    return x5.reshape(bsz, seq, nt, LANES).reshape(bsz, seq, d_model)
```

```python
import functools
import math

import jax
import jax.numpy as jnp
from jax import lax
from jax.experimental import pallas as pl
from jax.experimental.pallas import tpu as pltpu

F32 = jnp.float32
BF16 = jnp.bfloat16
I32 = jnp.int32

LANES = 128
MXU_DEPTH = 256
HEAD_DIM = 64
NEG_INF = -1e30
EPS = 1e-6

A_HEADS = 8
A_BRANCHES = ((128, 1), (512, 4), (2048, 16))
B_HEADS = 8
B_KV_HEADS = 2
GRID_W = 64
ROPE_THETA = 10000.0
C_HEADS = 16
C_KV_HEADS = 4
C_RADIUS = 128
NUM_BUCKETS = 32
MAX_EXACT = 8
MAX_DISTANCE = 1024
N_EXPERTS = 16
CAPACITY_FACTOR = 2

MIB = 1024 * 1024
NT_DIMS = (((1,), (1,)), ((), ()))


def _params(semantics, vmem_mib):
    return pltpu.CompilerParams(dimension_semantics=semantics, vmem_limit_bytes=vmem_mib * MIB)


def _load_rows(x_ref, tm, nt, tiled):
    if not tiled:
        return x_ref[...]
    return jnp.concatenate([x_ref[pl.ds(s, tm, stride=nt), :] for s in range(nt)], axis=1)


def _store_rows_tiled(o_ref, val, tm, nt):
    for s in range(nt):
        o_ref[pl.ds(s, tm, stride=nt), :] = val[:, s * LANES:(s + 1) * LANES]


def _proj_body(x_ref, g_ref, w_ref, bd_ref, gains_ref, cos_ref, sin_ref, *out_refs, groups, tm, nt, tiled):
    x = _load_rows(x_ref, tm, nt, tiled)
    ms = jnp.mean(x * x, axis=-1, keepdims=True)
    xn = (x * lax.rsqrt(ms + EPS) * g_ref[...]).astype(BF16)
    lane = lax.broadcasted_iota(I32, (1, LANES), 1)
    first16 = (lane % 32) < 16
    bd = bd_ref[...]
    for (c0, width, kind, grow, _), o_ref in zip(groups, out_refs):
        y = jnp.dot(xn, w_ref[:, c0:c0 + width], preferred_element_type=F32)
        if kind == "plain":
            o_ref[...] = y.astype(o_ref.dtype)
            continue
        if kind == "value_ones":
            for j in range(width // LANES):
                o_ref[:, j * LANES:(j + 1) * LANES] = jnp.where(
                    lane < HEAD_DIM, y[:, j * LANES:(j + 1) * LANES], 1.0).astype(o_ref.dtype)
            continue
        gain = gains_ref[grow:grow + 1, :]
        for j in range(width // LANES):
            yj = y[:, j * LANES:(j + 1) * LANES]
            sq = yj * yj
            hi = sq.astype(BF16)
            lo = (sq - hi.astype(F32)).astype(BF16)
            ss = jnp.dot(hi, bd, preferred_element_type=F32) + jnp.dot(lo, bd, preferred_element_type=F32)
            yn = yj * lax.rsqrt(ss * (1.0 / HEAD_DIM) + EPS) * gain
            if kind == "rope":
                partner = jnp.where(first16, pltpu.roll(yn, LANES - 16, 1), pltpu.roll(yn, 16, 1))
                yn = yn * cos_ref[...] + partner * sin_ref[...]
            o_ref[:, j * LANES:(j + 1) * LANES] = yn.astype(o_ref.dtype)


def _norm_proj(x, g, w, bd, gains, cos_t, sin_t, groups, *, seq, tiled, tm=512):
    d_model = w.shape[0]
    nt = d_model // LANES
    tokens = x.shape[0] // nt if tiled else x.shape[0]
    tm = min(tm, seq)
    nblk_seq = seq // tm
    x_spec = (pl.BlockSpec((tm * nt, LANES), lambda i: (i, 0)) if tiled
              else pl.BlockSpec((tm, d_model), lambda i: (i, 0)))
    const = lambda i: (0, 0)
    in_specs = [
        x_spec,
        pl.BlockSpec(g.shape, const),
        pl.BlockSpec(w.shape, const),
        pl.BlockSpec(bd.shape, const),
        pl.BlockSpec(gains.shape, const),
        pl.BlockSpec((tm, LANES), lambda i: (i % nblk_seq, 0)),
        pl.BlockSpec((tm, LANES), lambda i: (i % nblk_seq, 0)),
    ]
    out_shape = [jax.ShapeDtypeStruct((tokens, wd), dt) for (_, wd, _, _, dt) in groups]
    out_specs = [pl.BlockSpec((tm, wd), lambda i: (i, 0)) for (_, wd, _, _, _) in groups]
    return pl.pallas_call(
        functools.partial(_proj_body, groups=groups, tm=tm, nt=nt, tiled=tiled),
        grid=(tokens // tm,),
        in_specs=in_specs,
        out_specs=out_specs,
        out_shape=out_shape,
        compiler_params=_params(("parallel",), 48),
        name="norm_proj",
    )(x, g, w, bd, gains, cos_t, sin_t)


def _band_body(*refs, tq, qr, halo, length, has_state, final, has_sink):
    it = iter(refs)
    sink_ref = next(it) if has_sink else None
    q_ref, kp_ref, km_ref, kn_ref, vp_ref, vm_ref, vn_ref, bias_ref = (next(it) for _ in range(8))
    if has_state:
        m_in, l_in, a_in = next(it), next(it), next(it)
    if final:
        o_ref = next(it)
    else:
        m_out, l_out, a_out = next(it), next(it), next(it)

    cb = pl.program_id(1)
    blk = pl.program_id(2)
    win = qr + 2 * halo
    kext = jnp.concatenate([kp_ref[0], km_ref[0], kn_ref[0]], axis=0)
    vext = jnp.concatenate([vp_ref[0], vm_ref[0], vn_ref[0]], axis=0)
    lane = lax.broadcasted_iota(I32, (1, LANES), 1)
    lo = lane < HEAD_DIM
    col = lax.broadcasted_iota(I32, (1, win), 1)
    head_row = lax.broadcasted_iota(I32, (2 * qr, 1), 0) < qr
    bias = bias_ref[0]
    zero = jnp.zeros((), BF16)

    def split(tile):
        return jnp.concatenate([tile[:, 0:1], tile[:, HEAD_DIM:HEAD_DIM + 1]], axis=0)

    def merge(colv):
        return jnp.where(lo, colv[:qr], colv[qr:])

    for s in range(tq // qr):
        rows = slice(s * qr, (s + 1) * qr)
        qs = q_ref[0, rows, :]
        q2 = jnp.concatenate([jnp.where(lo, qs, zero), jnp.where(lo, zero, qs)], axis=0)
        kw = kext[s * qr:s * qr + win]
        vw = vext[s * qr:s * qr + win]
        sc = lax.dot_general(q2, kw, NT_DIMS, preferred_element_type=F32) + bias
        kpos = blk * tq + (s * qr - halo) + col
        sc = jnp.where((kpos >= 0) & (kpos < length), sc, NEG_INF)
        m_cur = jnp.max(sc, axis=1, keepdims=True)
        if has_state:
            m_prev = split(m_in[0, rows, :])
            l_prev = split(l_in[0, rows, :])
            m_new = jnp.maximum(m_prev, m_cur)
        else:
            m_new = m_cur
        if has_sink:
            sk = jnp.where(head_row, sink_ref[2 * cb], sink_ref[2 * cb + 1])
            m_new = jnp.maximum(m_new, sk)
        p = jnp.exp(sc - m_new)
        l_new = jnp.sum(p, axis=1, keepdims=True)
        pv = jnp.dot(p.astype(BF16), vw, preferred_element_type=F32)
        acc = merge(pv)
        if has_state:
            alpha = jnp.exp(m_prev - m_new)
            l_new = l_new + alpha * l_prev
            acc = acc + merge(alpha) * a_in[0, rows, :]
        if has_sink:
            l_new = l_new + jnp.exp(sk - m_new)
        if final:
            o_ref[0, rows, :] = (acc / merge(l_new)).astype(o_ref.dtype)
        else:
            m_out[0, rows, :] = jnp.broadcast_to(merge(m_new), (qr, LANES))
            l_out[0, rows, :] = jnp.broadcast_to(merge(l_new), (qr, LANES))
            a_out[0, rows, :] = acc


def _band_attention(q, k, v, bias, *, kv_div, bias_mod, qr, halo, state=None, final=True, sink=None, tq=512):
    bsz, length, qcols = q.shape
    ncb = qcols // LANES
    tq = min(tq, length)
    nq = length // tq
    per = tq // halo
    nh = length // halo
    win = qr + 2 * halo

    def qmap(b, cb, i, *_):
        return (b, i, cb)

    def kmain(b, cb, i, *_):
        return (b, i, cb // kv_div)

    def kprev(b, cb, i, *_):
        return (b, jnp.maximum(i * per - 1, 0), cb // kv_div)

    def knext(b, cb, i, *_):
        return (b, jnp.minimum((i + 1) * per, nh - 1), cb // kv_div)

    def bmap(b, cb, i, *_):
        return (cb % bias_mod, 0, 0)

    main = lambda m: pl.BlockSpec((1, tq, LANES), m)
    edge = lambda m: pl.BlockSpec((1, halo, LANES), m)
    in_specs = [main(qmap), edge(kprev), main(kmain), edge(knext), edge(kprev), main(kmain), edge(knext),
                pl.BlockSpec((1, 2 * qr, win), bmap)]
    args = [q, k, k, k, v, v, v, bias]
    if state is not None:
        in_specs += [main(qmap)] * 3
        args += list(state)
    if final:
        out_shape = jax.ShapeDtypeStruct(q.shape, BF16)
        out_specs = main(qmap)
    else:
        out_shape = [jax.ShapeDtypeStruct(q.shape, F32)] * 3
        out_specs = [main(qmap)] * 3
    has_sink = sink is not None
    grid_spec = pltpu.PrefetchScalarGridSpec(
        num_scalar_prefetch=1 if has_sink else 0,
        grid=(bsz, ncb, nq),
        in_specs=in_specs,
        out_specs=out_specs,
    )
    body = functools.partial(_band_body, tq=tq, qr=qr, halo=halo, length=length,
                             has_state=state is not None, final=final, has_sink=has_sink)
    call = pl.pallas_call(body, grid_spec=grid_spec, out_shape=out_shape,
                          compiler_params=_params(("parallel", "parallel", "parallel"), 48),
                          name="band_attention")
    return call(sink, *args) if has_sink else call(*args)


DIL_QR = 128


def _dilated_body(*refs, branches, tq, halo, seq, has_sink):
    sink_ref = refs[0] if has_sink else None
    q_ref, kp_ref, km_ref, kn_ref, vp_ref, vm_ref, vn_ref = refs[has_sink:has_sink + 7]
    rest = refs[has_sink + 7:]
    nb = len(branches)
    bias_refs = rest[:nb]
    o_ref = rest[nb]
    kext, vext, s_a, s_b, p_a, p_b = rest[nb + 1:nb + 7]
    stats = rest[nb + 7:]
    blk = pl.program_id(2)
    qr = DIL_QR
    kext[0:halo, :] = kp_ref[0]
    kext[halo:halo + tq, :] = km_ref[0]
    kext[halo + tq:halo + tq + halo, :] = kn_ref[0]
    vext[0:halo, :] = vp_ref[0]
    vext[halo:halo + tq, :] = vm_ref[0]
    vext[halo + tq:halo + tq + halo, :] = vn_ref[0]
    p_b[...] = jnp.zeros(p_b.shape, BF16)
    lane = lax.broadcasted_iota(I32, (1, LANES), 1)
    lo = lane < HEAD_DIM
    zero = jnp.zeros((), BF16)
    nunits = tq // qr

    for bi, (dil, radius) in enumerate(branches):
        win = qr + 2 * radius
        m_st, l_st, a_st = stats[3 * bi:3 * bi + 3]
        bias_ref = bias_refs[bi]
        upr = nunits // dil
        off = halo - radius * dil
        col = lax.broadcasted_iota(I32, (1, win), 1)

        def first_row(u, dil=dil, upr=upr):
            u = jnp.clip(u, 0, nunits - 1)
            start = u // upr + (u % upr) * (qr * dil)
            return pl.multiple_of(start, qr) if dil == 1 else start

        def rows(start, n, dil=dil):
            return pl.ds(start, n) if dil == 1 else pl.ds(start, n, stride=dil)

        def scores(u, s_ref, first_row=first_row, rows=rows, off=off, win=win):
            start = first_row(u)
            qs = q_ref[0, rows(start, qr), :].astype(BF16)
            q2 = jnp.concatenate([jnp.where(lo, qs, zero), jnp.where(lo, zero, qs)], axis=0)
            kw = kext[rows(off + start, win), :].astype(BF16)
            s_ref[:, 0:win] = lax.dot_general(q2, kw, NT_DIMS, preferred_element_type=F32)

        def softmax(u, s_ref, p_ref, first_row=first_row, rows=rows, win=win, dil=dil, radius=radius,
                    bias_ref=bias_ref, m_st=m_st, l_st=l_st, col=col):
            start = first_row(u)
            sc = s_ref[:, 0:win] + bias_ref[0]
            kpos = blk * tq + start - radius * dil + dil * col
            sc = jnp.where((kpos >= 0) & (kpos < seq), sc, NEG_INF)
            m = jnp.max(sc, axis=1, keepdims=True)
            p = jnp.exp2(sc - m)
            l = jnp.sum(p, axis=1, keepdims=True)
            p_ref[:, 0:win] = p.astype(BF16)
            m_st[rows(start, qr), :] = jnp.where(lo, m[:qr], m[qr:])
            l_st[rows(start, qr), :] = jnp.where(lo, l[:qr], l[qr:])

        def values(u, p_ref, first_row=first_row, rows=rows, off=off, win=win, a_st=a_st):
            start = first_row(u)
            vw = vext[rows(off + start, win), :].astype(BF16)
            pv = jnp.dot(p_ref[:, 0:win], vw, preferred_element_type=F32)
            a_st[rows(start, qr), :] = jnp.where(lo, pv[:qr], pv[qr:])

        scores(0, s_a)

        def pair(i, carry, scores=scores, softmax=softmax, values=values):
            u = 2 * i
            scores(u + 1, s_b)
            values(u - 1, p_b)
            softmax(u, s_a, p_a)
            scores(u + 2, s_a)
            values(u, p_a)
            softmax(u + 1, s_b, p_b)
            return carry

        lax.fori_loop(0, nunits // 2, pair, 0)
        values(nunits - 1, p_b)

    if has_sink:
        cb = pl.program_id(1)
        sink_row = jnp.where(lo, sink_ref[2 * cb], sink_ref[2 * cb + 1])
    chunk = 256
    for c in range(tq // chunk):
        rws = slice(c * chunk, (c + 1) * chunk)
        ms = [stats[3 * bi][rws, :] for bi in range(nb)]
        mx = functools.reduce(jnp.maximum, ms)
        num = None
        den = None
        if has_sink:
            mx = jnp.maximum(mx, sink_row)
            den = jnp.exp2(sink_row - mx)
        for bi in range(nb):
            w = jnp.exp2(ms[bi] - mx)
            n_b = w * stats[3 * bi + 2][rws, :]
            d_b = w * stats[3 * bi + 1][rws, :]
            num = n_b if num is None else num + n_b
            den = d_b if den is None else den + d_b
        o_ref[0, rws, :] = (num / den).astype(o_ref.dtype)


def _dilated_attention(q, k, v, biases, branches, *, kv_div=1, sink=None, tq=2048, name="dilated_attention"):
    bsz, seq, cols = q.shape
    ncb = cols // LANES
    halo = max(d * r for d, r in branches)
    tq = min(tq, seq)
    assert tq % halo == 0 and seq % tq == 0 and (tq // DIL_QR) % 2 == 0
    assert all((tq // DIL_QR) % d == 0 for d, _ in branches)
    per = tq // halo
    nh = seq // halo
    maxwin = max(DIL_QR + 2 * r for _, r in branches)

    qmain = pl.BlockSpec((1, tq, LANES), lambda b, cb, i, *_: (b, i, cb))
    main = pl.BlockSpec((1, tq, LANES), lambda b, cb, i, *_: (b, i, cb // kv_div))
    prev = pl.BlockSpec((1, halo, LANES), lambda b, cb, i, *_: (b, jnp.maximum(i * per - 1, 0), cb // kv_div))
    nxt = pl.BlockSpec((1, halo, LANES), lambda b, cb, i, *_: (b, jnp.minimum((i + 1) * per, nh - 1), cb // kv_div))
    bias_specs = [pl.BlockSpec((1,) + bz.shape[1:], lambda b, cb, i, *_: (cb, 0, 0)) for bz in biases]
    ext = tq + 2 * halo
    scratch = ([pltpu.VMEM((ext, LANES), k.dtype)] * 2
               + [pltpu.VMEM((2 * DIL_QR, maxwin), F32)] * 2
               + [pltpu.VMEM((2 * DIL_QR, maxwin), BF16)] * 2
               + [pltpu.VMEM((tq, LANES), F32)] * (3 * len(branches)))
    has_sink = sink is not None
    grid_spec = pltpu.PrefetchScalarGridSpec(
        num_scalar_prefetch=int(has_sink),
        grid=(bsz, ncb, seq // tq),
        in_specs=[qmain, prev, main, nxt, prev, main, nxt] + bias_specs,
        out_specs=qmain,
        scratch_shapes=scratch,
    )
    call = pl.pallas_call(
        functools.partial(_dilated_body, branches=tuple(branches), tq=tq, halo=halo, seq=seq, has_sink=has_sink),
        grid_spec=grid_spec,
        out_shape=jax.ShapeDtypeStruct(q.shape, BF16),
        compiler_params=_params(("parallel", "parallel", "parallel"), 48),
        name=name,
    )
    args = (q, k, k, k, v, v, v, *biases)
    return call(sink, *args) if has_sink else call(*args)


def _dense_body(q_ref, k_ref, v_ref, o_ref, q4_ref, s_a, s_b, p_a, p_b, al_a, al_b, m_ref, acc_ref, *, tq, tk, nk, rb):
    lane = lax.broadcasted_iota(I32, (1, LANES), 1)
    lo = lane < HEAD_DIM
    zero = jnp.zeros((), BF16)
    for h in range(4):
        blk = q_ref[0, :, (h // 2) * LANES:(h // 2 + 1) * LANES]
        q4_ref[h * tq:(h + 1) * tq, :] = jnp.where(lo, blk, zero) if h % 2 == 0 else jnp.where(lo, zero, blk)
    m_ref[...] = jnp.full(m_ref.shape, NEG_INF, F32)
    acc_ref[...] = jnp.zeros(acc_ref.shape, F32)
    p_b[...] = jnp.zeros(p_b.shape, BF16)
    al_b[...] = jnp.ones(al_b.shape, F32)
    nct = tk // LANES

    def scores(j, s_ref):
        start = pl.multiple_of(jnp.minimum(j, nk - 1) * tk, tk)
        s_ref[...] = lax.dot_general(q4_ref[...], k_ref[0, pl.ds(start, tk), :], NT_DIMS, preferred_element_type=F32)

    def weighted_values(j, p_ref, al_ref):
        start = pl.multiple_of(jnp.maximum(j, 0) * tk, tk)
        pv = None
        for c in range(tk // MXU_DEPTH):
            part = jnp.dot(p_ref[:, c * MXU_DEPTH:(c + 1) * MXU_DEPTH],
                           v_ref[0, pl.ds(pl.multiple_of(start + c * MXU_DEPTH, MXU_DEPTH), MXU_DEPTH), :],
                           preferred_element_type=F32)
            pv = part if pv is None else pv + part
        acc_ref[...] = al_ref[...] * acc_ref[...] + pv

    def softmax_update(s_ref, p_ref, al_ref):
        for r in range(4 * tq // rb):
            rows = slice(r * rb, (r + 1) * rb)
            mt = functools.reduce(jnp.maximum, [s_ref[rows, c * LANES:(c + 1) * LANES] for c in range(nct)])
            m_prev = m_ref[rows, :]
            m_new = jnp.maximum(m_prev, jnp.max(mt, axis=1, keepdims=True))
            al_ref[rows, :] = jnp.exp2(m_prev - m_new)
            m_ref[rows, :] = m_new
        for r in range(4 * tq // rb):
            rows = slice(r * rb, (r + 1) * rb)
            m_new = m_ref[rows, :]
            for c in range(nct):
                cols = slice(c * LANES, (c + 1) * LANES)
                p_ref[rows, cols] = jnp.exp2(s_ref[rows, cols] - m_new).astype(BF16)

    scores(0, s_a)

    def pair(i, carry):
        j = 2 * i
        scores(j + 1, s_b)
        weighted_values(j - 1, p_b, al_b)
        softmax_update(s_a, p_a, al_a)
        scores(j + 2, s_a)
        weighted_values(j, p_a, al_a)
        softmax_update(s_b, p_b, al_b)
        return carry

    lax.fori_loop(0, nk // 2, pair, 0)
    weighted_values(nk - 1, p_b, al_b)
    acc = acc_ref[...]
    o = acc / jnp.where(lo, pltpu.roll(acc, HEAD_DIM, 1), acc)
    for hp in range(2):
        even = o[(2 * hp) * tq:(2 * hp + 1) * tq]
        odd = pltpu.roll(o[(2 * hp + 1) * tq:(2 * hp + 2) * tq], HEAD_DIM, 1)
        o_ref[0, :, hp * LANES:(hp + 1) * LANES] = jnp.where(lo, even, odd).astype(o_ref.dtype)


def _dense_attention(q, k, v, *, tq=128, tk=512, rb=32):
    bsz, seq, _ = q.shape
    tq = min(tq, seq)
    tk = min(tk, seq // 2)
    assert seq % (2 * tk) == 0 and seq % tq == 0
    ngroups = k.shape[2] // LANES
    return pl.pallas_call(
        functools.partial(_dense_body, tq=tq, tk=tk, nk=seq // tk, rb=min(rb, 4 * tq)),
        grid=(bsz, ngroups, seq // tq),
        in_specs=[
            pl.BlockSpec((1, tq, 2 * LANES), lambda b, g, i: (b, i, g)),
            pl.BlockSpec((1, seq, LANES), lambda b, g, i: (b, 0, g)),
            pl.BlockSpec((1, seq, LANES), lambda b, g, i: (b, 0, g)),
        ],
        out_specs=pl.BlockSpec((1, tq, 2 * LANES), lambda b, g, i: (b, i, g)),
        out_shape=jax.ShapeDtypeStruct(q.shape, BF16),
        scratch_shapes=(
            [pltpu.VMEM((4 * tq, LANES), BF16)]
            + [pltpu.VMEM((4 * tq, tk), F32)] * 2
            + [pltpu.VMEM((4 * tq, tk), BF16)] * 2
            + [pltpu.VMEM((4 * tq, LANES), F32)] * 4
        ),
        compiler_params=_params(("parallel", "parallel", "parallel"), 48),
        name="dense_attention",
    )(q, k, v)


def _outproj_body(*refs, n_in, tm, nt, tiled):
    o_refs = refs[:n_in]
    w_ref, x_ref, g_ref, wr_ref, x2_ref, hn_ref, aff_ref = refs[n_in:]
    y = None
    c0 = 0
    for o_ref in o_refs:
        wd = o_ref.shape[1]
        part = jnp.dot(o_ref[...], w_ref[c0:c0 + wd, :], preferred_element_type=F32)
        y = part if y is None else y + part
        c0 += wd
    x2 = _load_rows(x_ref, tm, nt, tiled) + y
    ms = jnp.mean(x2 * x2, axis=-1, keepdims=True)
    hn = x2 * lax.rsqrt(ms + EPS) * g_ref[...]
    _store_rows_tiled(x2_ref, x2, tm, nt)
    _store_rows_tiled(hn_ref, hn, tm, nt)
    wr = wr_ref[...]
    wr_hi = wr.astype(BF16)
    wr_lo = (wr - wr_hi.astype(F32)).astype(BF16)
    hn_hi = hn.astype(BF16)
    hn_lo = (hn - hn_hi.astype(F32)).astype(BF16)
    lt = (lax.dot_general(wr_hi, hn_hi, NT_DIMS, preferred_element_type=F32)
          + lax.dot_general(wr_lo, hn_hi, NT_DIMS, preferred_element_type=F32)
          + lax.dot_general(wr_hi, hn_lo, NT_DIMS, preferred_element_type=F32))
    ex = jnp.exp(lt - jnp.max(lt, axis=0, keepdims=True))
    aff_ref[0] = ex / jnp.sum(ex, axis=0, keepdims=True)


def _out_proj(o_parts, w, x, g, wr_t, *, bsz, seq, tiled, tm=512):
    d_model = w.shape[1]
    nt = d_model // LANES
    tokens = bsz * seq
    tm = min(tm, seq)
    nblk_seq = seq // tm
    nexp = wr_t.shape[0]
    const = lambda i: (0, 0)
    x_spec = (pl.BlockSpec((tm * nt, LANES), lambda i: (i, 0)) if tiled
              else pl.BlockSpec((tm, d_model), lambda i: (i, 0)))
    in_specs = [pl.BlockSpec((tm, o.shape[1]), lambda i: (i, 0)) for o in o_parts]
    in_specs += [pl.BlockSpec(w.shape, const), x_spec, pl.BlockSpec(g.shape, const), pl.BlockSpec(wr_t.shape, const)]
    tiled_spec = pl.BlockSpec((tm * nt, LANES), lambda i: (i, 0))
    return pl.pallas_call(
        functools.partial(_outproj_body, n_in=len(o_parts), tm=tm, nt=nt, tiled=tiled),
        grid=(tokens // tm,),
        in_specs=in_specs,
        out_specs=[tiled_spec, tiled_spec,
                   pl.BlockSpec((1, nexp, tm), lambda i: (i // nblk_seq, 0, i % nblk_seq))],
        out_shape=[jax.ShapeDtypeStruct((tokens * nt, LANES), F32),
                   jax.ShapeDtypeStruct((tokens * nt, LANES), F32),
                   jax.ShapeDtypeStruct((bsz, nexp, seq), F32)],
        compiler_params=_params(("parallel",), 48),
        name="out_proj_router",
    )(*o_parts, w, x, g, wr_t)


INDEX_CHUNK = 128


def _block_cumsum(x, tri, strict_lower):
    within = jnp.dot(x.astype(BF16), tri, preferred_element_type=F32)
    total = jnp.broadcast_to(within[:, LANES - 1:LANES], within.shape)
    before = jnp.dot(strict_lower, total.astype(BF16), preferred_element_type=F32)
    return within + before, before + total


def _route_body(aff_ref, aff4_ref, idx_ref, thr_ref, need_ref, *, seq, cap, nexp):
    e = pl.program_id(1)
    nblk = seq // LANES

    @pl.when(e == 0)
    def _():
        bits = pltpu.bitcast(aff_ref[0], I32)

        def search(_, c):
            lo, hi = c
            mid = lo + ((hi - lo + 1) >> 1)
            cnt = jnp.sum(jnp.where(bits >= mid, 1.0, 0.0), axis=1, keepdims=True)
            ok = cnt >= cap
            return jnp.where(ok, mid, lo), jnp.where(ok, hi, mid - 1)

        lo0 = jnp.zeros((nexp, 1), I32)
        hi0 = jnp.full((nexp, 1), 0x7F800000, I32)
        thr, _ = lax.fori_loop(0, 32, search, (lo0, hi0))
        need = cap - jnp.sum(jnp.where(bits > thr, 1.0, 0.0), axis=1, keepdims=True)
        thr_ref[...] = jnp.broadcast_to(thr, (nexp, LANES))
        need_ref[...] = jnp.broadcast_to(need, (nexp, LANES))

    bits = pltpu.bitcast(aff4_ref[0, 0], I32)
    thr = thr_ref[pl.ds(e, 1), :]
    need = need_ref[pl.ds(e, 1), :]
    lane = lax.broadcasted_iota(I32, (1, LANES), 1)
    r = lax.broadcasted_iota(I32, (LANES, LANES), 0)
    c = lax.broadcasted_iota(I32, (LANES, LANES), 1)
    tri = jnp.where(r <= c, 1.0, 0.0).astype(BF16)
    eye = r == c
    strict_lower = jnp.where(lax.broadcasted_iota(I32, (nblk, nblk), 1) < lax.broadcasted_iota(I32, (nblk, nblk), 0),
                             1.0, 0.0).astype(BF16)
    eye_rows = lax.broadcasted_iota(I32, (nblk, LANES), 0) == lax.broadcasted_iota(I32, (nblk, LANES), 1)
    gt = bits > thr
    eq = bits == thr
    eq_rank, _ = _block_cumsum(jnp.where(eq, 1.0, 0.0), tri, strict_lower)
    sel = gt | (eq & (eq_rank <= need))
    cs, row_end = _block_cumsum(jnp.where(sel, 1.0, 0.0), tri, strict_lower)
    end_lanes = jnp.sum(jnp.where(eye_rows, row_end, 0.0), axis=0, keepdims=True)
    end_lanes = jnp.where(lane < nblk, end_lanes, 2.0 * seq)
    hi = jnp.floor(cs * (1.0 / LANES))
    lo = cs - LANES * hi
    pad = jnp.zeros((LANES - nblk, LANES), F32)
    hi = jnp.concatenate([hi, pad], axis=0).astype(BF16)
    lo = jnp.concatenate([lo, pad], axis=0).astype(BF16)
    lane_f = lane.astype(F32)
    for cc in range(cap // INDEX_CHUNK):
        slot = (lax.broadcasted_iota(I32, (INDEX_CHUNK, 1), 0) + cc * INDEX_CHUNK).astype(F32)
        krow = jnp.sum(jnp.where(end_lanes <= slot, 1.0, 0.0), axis=1, keepdims=True)
        onehot = jnp.where(lane_f == krow, 1.0, 0.0).astype(BF16)
        counts = (LANES * jnp.dot(onehot, hi, preferred_element_type=F32)
                  + jnp.dot(onehot, lo, preferred_element_type=F32))
        inrow = jnp.sum(jnp.where(counts <= slot, 1.0, 0.0), axis=1, keepdims=True)
        token = LANES * krow + inrow
        idx_ref[0, :, cc * INDEX_CHUNK:(cc + 1) * INDEX_CHUNK] = jnp.sum(
            jnp.where(eye, token, 0.0), axis=0, keepdims=True).astype(I32)


def _route(aff_t, cap):
    bsz, nexp, seq = aff_t.shape
    nblk = seq // LANES
    assert nblk <= LANES and cap % INDEX_CHUNK == 0
    return pl.pallas_call(
        functools.partial(_route_body, seq=seq, cap=cap, nexp=nexp),
        grid=(bsz, nexp),
        in_specs=[pl.BlockSpec((1, nexp, seq), lambda b, e: (b, 0, 0)),
                  pl.BlockSpec((1, 1, nblk, LANES), lambda b, e: (b, e, 0, 0))],
        out_specs=pl.BlockSpec((1, 1, cap), lambda b, e: (b * nexp + e, 0, 0)),
        out_shape=jax.ShapeDtypeStruct((bsz * nexp, 1, cap), I32),
        scratch_shapes=[pltpu.VMEM((nexp, LANES), I32), pltpu.VMEM((nexp, LANES), F32)],
        compiler_params=_params(("parallel", "arbitrary"), 48),
        name="route",
    )(aff_t, aff_t.reshape(bsz, nexp, nblk, LANES))


ROW_UNROLL = 8


def _gather_body(idx_ref, h_hbm, x_ref, hbuf, xbuf, sem, *, cap, nt):
    b = pl.program_id(0)
    e = pl.program_id(1)

    @pl.when(e == 0)
    def _():
        cp = pltpu.make_async_copy(h_hbm.at[b], hbuf, sem)
        cp.start()
        cp.wait()

    def rows(i, carry):
        for u in range(ROW_UNROLL):
            c = i * ROW_UNROLL + u
            t = idx_ref[0, 0, c]
            xbuf[pl.ds(pl.multiple_of(c * nt, nt), nt), :] = hbuf[pl.ds(pl.multiple_of(t * nt, nt), nt), :]
        return carry

    lax.fori_loop(0, cap // ROW_UNROLL, rows, 0)
    for s in range(nt):
        x_ref[0, :, s * LANES:(s + 1) * LANES] = xbuf[pl.ds(s, cap, stride=nt), :].astype(x_ref.dtype)


def _gather(idx, h_tiled, *, bsz, nexp, seq, cap, nt):
    return pl.pallas_call(
        functools.partial(_gather_body, cap=cap, nt=nt),
        grid=(bsz, nexp),
        in_specs=[pl.BlockSpec((1, 1, cap), lambda b, e: (b * nexp + e, 0, 0), memory_space=pltpu.SMEM),
                  pl.BlockSpec(memory_space=pl.ANY)],
        out_specs=pl.BlockSpec((1, cap, nt * LANES), lambda b, e: (e, b, 0)),
        out_shape=jax.ShapeDtypeStruct((nexp, bsz * cap, nt * LANES), BF16),
        scratch_shapes=[pltpu.VMEM((seq * nt, LANES), F32), pltpu.VMEM((cap * nt, LANES), F32),
                        pltpu.SemaphoreType.DMA(())],
        compiler_params=_params(("arbitrary", "arbitrary"), 56),
        name="moe_gather",
    )(idx, h_tiled)


def _ffn_body(x_ref, wg_ref, wu_ref, wd_ref, y_ref, acc_ref, *, cap, nt):
    f = pl.program_id(2)
    x = x_ref[0]
    g = jnp.dot(x, wg_ref[0].astype(BF16), preferred_element_type=F32)
    u = jnp.dot(x, wu_ref[0].astype(BF16), preferred_element_type=F32)
    act = (g * (1.0 / (1.0 + jnp.exp(-g))) * u).astype(BF16)
    y = jnp.dot(act, wd_ref[0].astype(BF16), preferred_element_type=F32)

    @pl.when(f == 0)
    def _():
        acc_ref[...] = y

    @pl.when(f > 0)
    def _():
        acc_ref[...] += y

    @pl.when(f == pl.num_programs(2) - 1)
    def _():
        _store_rows_tiled(y_ref.at[0], acc_ref[...], cap, nt)


def _ffn(x, wg, wu, wd, *, bsz, cap, tf=512):
    nexp, d_model, d_exp = wg.shape
    nt = d_model // LANES
    tf = min(tf, d_exp)
    return pl.pallas_call(
        functools.partial(_ffn_body, cap=cap, nt=nt),
        grid=(nexp, bsz, d_exp // tf),
        in_specs=[pl.BlockSpec((1, cap, d_model), lambda e, b, f: (e, b, 0)),
                  pl.BlockSpec((1, d_model, tf), lambda e, b, f: (e, 0, f)),
                  pl.BlockSpec((1, d_model, tf), lambda e, b, f: (e, 0, f)),
                  pl.BlockSpec((1, tf, d_model), lambda e, b, f: (e, f, 0))],
        out_specs=pl.BlockSpec((1, cap * nt, LANES), lambda e, b, f: (b * nexp + e, 0, 0)),
        out_shape=jax.ShapeDtypeStruct((bsz * nexp, cap * nt, LANES), F32),
        scratch_shapes=[pltpu.VMEM((cap, d_model), F32)],
        compiler_params=_params(("parallel", "parallel", "arbitrary"), 56),
        name="moe_ffn",
    )(x, wg, wu, wd)


def _combine_body(idx_ref, aff_ref, x_hbm, y_ref, o_hbm, acc, sem, *, cap, nt, nexp):
    b = pl.program_id(0)
    e = pl.program_id(1)

    @pl.when(e == 0)
    def _():
        cp = pltpu.make_async_copy(x_hbm.at[b], acc, sem)
        cp.start()
        cp.wait()

    def rows(i, carry):
        dst, vals = [], []
        for u in range(ROW_UNROLL):
            c = i * ROW_UNROLL + u
            t = idx_ref[0, 0, c]
            gate = aff_ref[0, 0, t]
            d = pl.ds(pl.multiple_of(t * nt, nt), nt)
            dst.append(d)
            vals.append(acc[d, :] + gate * y_ref[0, pl.ds(pl.multiple_of(c * nt, nt), nt), :])
        for d, val in zip(dst, vals):
            acc[d, :] = val
        return carry

    lax.fori_loop(0, cap // ROW_UNROLL, rows, 0)

    @pl.when(e == nexp - 1)
    def _():
        cp = pltpu.make_async_copy(acc, o_hbm.at[b], sem)
        cp.start()
        cp.wait()


def _combine(idx, aff_rows, x_tiled, y_tiled, *, bsz, nexp, seq, cap, nt):
    return pl.pallas_call(
        functools.partial(_combine_body, cap=cap, nt=nt, nexp=nexp),
        grid=(bsz, nexp),
        in_specs=[pl.BlockSpec((1, 1, cap), lambda b, e: (b * nexp + e, 0, 0), memory_space=pltpu.SMEM),
                  pl.BlockSpec((1, 1, seq), lambda b, e: (b * nexp + e, 0, 0), memory_space=pltpu.SMEM),
                  pl.BlockSpec(memory_space=pl.ANY),
                  pl.BlockSpec((1, cap * nt, LANES), lambda b, e: (b * nexp + e, 0, 0))],
        out_specs=pl.BlockSpec(memory_space=pl.ANY),
        out_shape=jax.ShapeDtypeStruct(x_tiled.shape, F32),
        scratch_shapes=[pltpu.VMEM((seq * nt, LANES), F32), pltpu.SemaphoreType.DMA(())],
        compiler_params=_params(("arbitrary", "arbitrary"), 56),
        name="moe_combine",
    )(idx, aff_rows, x_tiled, y_tiled)


def _moe(x2_t, hn_t, aff_t, wg, wu, wd, *, bsz, seq):
    nexp = aff_t.shape[1]
    nt = wg.shape[1] // LANES
    cap = CAPACITY_FACTOR * seq // nexp
    idx = _route(aff_t, cap)
    xg = _gather(idx, hn_t.reshape(bsz, seq * nt, LANES), bsz=bsz, nexp=nexp, seq=seq, cap=cap, nt=nt)
    y = _ffn(xg, wg, wu, wd, bsz=bsz, cap=cap)
    out = _combine(idx, aff_t.reshape(bsz * nexp, 1, seq), x2_t.reshape(bsz, seq * nt, LANES), y,
                   bsz=bsz, nexp=nexp, seq=seq, cap=cap, nt=nt)
    return out.reshape(bsz * seq * nt, LANES)


def _t5_bucket(rel):
    half = NUM_BUCKETS // 2
    ret = jnp.where(rel > 0, half, 0)
    n = jnp.abs(rel)
    nf = jnp.maximum(n, 1).astype(F32)
    large = MAX_EXACT + (jnp.log(nf / MAX_EXACT) / math.log(MAX_DISTANCE / MAX_EXACT)
                         * (half - MAX_EXACT)).astype(I32)
    large = jnp.minimum(large, half - 1)
    return ret + jnp.where(n < MAX_EXACT, n, large)


def _band_bias(rel_bias, nheads, qr, halo, radius, dil):
    rel = jnp.arange(qr + 2 * halo)[None, :] - halo - jnp.arange(qr)[:, None]
    onehot = (_t5_bucket(rel * dil)[None] == jnp.arange(NUM_BUCKETS)[:, None, None]).astype(F32)
    tile = jnp.sum(rel_bias[:nheads].astype(F32)[:, :, None, None] * onehot[None], axis=1)
    tile = jnp.where((jnp.abs(rel) <= radius)[None], tile, NEG_INF)
    return tile.reshape(nheads // 2, 2 * qr, qr + 2 * halo)


def _tile_gain(g, scale=1.0):
    return jnp.tile(g.astype(F32) * scale, LANES // HEAD_DIM)[None, :]


def _dup_heads(w, nheads):
    d = w.shape[0]
    w4 = w.reshape(d, nheads, 1, HEAD_DIM)
    return jnp.broadcast_to(w4, (d, nheads, 2, HEAD_DIM)).reshape(d, nheads * 2 * HEAD_DIM)


def _rope_tables(seq):
    half = HEAD_DIM // 2
    inv = ROPE_THETA ** (-jnp.arange(0, half, 2, dtype=F32) / half)
    t = jnp.arange(seq)
    row = (t // GRID_W).astype(F32)[:, None] * inv
    colp = (t % GRID_W).astype(F32)[:, None] * inv
    cos64 = jnp.concatenate([jnp.cos(row), jnp.cos(row), jnp.cos(colp), jnp.cos(colp)], axis=1)
    sin64 = jnp.concatenate([-jnp.sin(row), jnp.sin(row), -jnp.sin(colp), jnp.sin(colp)], axis=1)
    return jnp.tile(cos64, (1, 2)), jnp.tile(sin64, (1, 2))


def _block_diag_ones():
    r = jnp.arange(LANES)
    return (r[:, None] // HEAD_DIM == r[None, :] // HEAD_DIM).astype(BF16)


def kernel(x, rel_bias, l0_norm_attn, l0_w_in, l0_a_qnorm, l0_a_knorm, l0_b_qnorm, l0_b_knorm, l0_w_out,
           l0_norm_ffn, l0_router, l0_w_gate, l0_w_up, l0_w_down, l1_norm_attn, l1_w_in, l1_c_qnorm, l1_c_knorm,
           l1_sink, l1_w_out, l1_norm_ffn, l1_router, l1_w_gate, l1_w_up, l1_w_down):
    bsz, seq, d_model = x.shape
    nt = d_model // LANES
    scale = 1.0 / math.sqrt(HEAD_DIM)
    a_w = A_HEADS * HEAD_DIM
    bq_w = B_HEADS * HEAD_DIM
    bkv_w = B_KV_HEADS * HEAD_DIM
    cq_w = C_HEADS * HEAD_DIM
    ckv_w = C_KV_HEADS * HEAD_DIM
    bd = _block_diag_ones()
    cos_t, sin_t = _rope_tables(seq)

    c = 3 * a_w + bq_w
    w0 = jnp.concatenate([l0_w_in[:, :c], _dup_heads(l0_w_in[:, c:c + bkv_w], B_KV_HEADS),
                          _dup_heads(l0_w_in[:, c + bkv_w:], B_KV_HEADS)], axis=1).astype(BF16)
    log2e = math.log2(math.e)
    gains0 = jnp.concatenate([_tile_gain(l0_a_qnorm, scale * log2e), _tile_gain(l0_a_knorm),
                              _tile_gain(l0_b_qnorm, scale * log2e), _tile_gain(l0_b_knorm)], axis=0)
    groups0 = ((0, a_w, "norm", 0, F32), (a_w, a_w, "norm", 1, F32), (2 * a_w, a_w, "plain", 0, F32),
               (3 * a_w, bq_w, "rope", 2, BF16), (c, 2 * bkv_w, "rope", 3, BF16),
               (c + 2 * bkv_w, 2 * bkv_w, "value_ones", 0, BF16))
    qa, ka, va, qb, kb, vb = _norm_proj(x.reshape(bsz * seq, d_model), l0_norm_attn[None, :], w0, bd, gains0,
                                        cos_t, sin_t, groups0, seq=seq, tiled=False)
    branches = tuple((dil, window // (2 * dil)) for window, dil in A_BRANCHES)
    biases = [_band_bias(rel_bias, A_HEADS, DIL_QR, radius, radius, dil) * log2e for dil, radius in branches]
    shp = lambda t: t.reshape(bsz, seq, t.shape[-1])
    oa = _dilated_attention(shp(qa), shp(ka), shp(va), biases, branches).reshape(bsz * seq, a_w)
    ob = _dense_attention(qb.reshape(bsz, seq, bq_w), kb.reshape(bsz, seq, 2 * bkv_w),
                          vb.reshape(bsz, seq, 2 * bkv_w)).reshape(bsz * seq, bq_w)
    x2, hn, aff = _out_proj([oa, ob], l0_w_out.astype(BF16), x.reshape(bsz * seq, d_model), l0_norm_ffn[None, :],
                            l0_router.T, bsz=bsz, seq=seq, tiled=False)
    x3 = _moe(x2, hn, aff, l0_w_gate, l0_w_up, l0_w_down, bsz=bsz, seq=seq)

    w1 = jnp.concatenate([l1_w_in[:, :cq_w], _dup_heads(l1_w_in[:, cq_w:cq_w + ckv_w], C_KV_HEADS),
                          _dup_heads(l1_w_in[:, cq_w + ckv_w:], C_KV_HEADS)], axis=1).astype(BF16)
    gains1 = jnp.concatenate([_tile_gain(l1_c_qnorm, scale * log2e), _tile_gain(l1_c_knorm)], axis=0)
    groups1 = ((0, cq_w, "norm", 0, BF16), (cq_w, 2 * ckv_w, "norm", 1, BF16),
               (cq_w + 2 * ckv_w, 2 * ckv_w, "plain", 0, BF16))
    qc, kc, vc = _norm_proj(x3, l1_norm_attn[None, :], w1, bd, gains1, cos_t, sin_t, groups1, seq=seq, tiled=True)
    bias_c = _band_bias(rel_bias, C_HEADS, DIL_QR, C_RADIUS, C_RADIUS, 1) * log2e
    oc = _dilated_attention(shp(qc), shp(kc), shp(vc), [bias_c], ((1, C_RADIUS),), kv_div=C_HEADS // C_KV_HEADS // 2,
                            sink=l1_sink.astype(F32) * log2e, name="window_attention").reshape(bsz * seq, cq_w)
    x4, hn1, aff1 = _out_proj([oc], l1_w_out.astype(BF16), x3, l1_norm_ffn[None, :], l1_router.T,
                              bsz=bsz, seq=seq, tiled=True)
    x5 = _moe(x4, hn1, aff1, l1_w_gate, l1_w_up, l1_w_down, bsz=bsz, seq=seq)
    return x5.reshape(bsz, seq, nt, LANES).reshape(bsz, seq, d_model)
```

```python
import functools
import math

import jax
import jax.numpy as jnp
from jax import lax
from jax.experimental import pallas as pl
from jax.experimental.pallas import tpu as pltpu

F32 = jnp.float32
BF16 = jnp.bfloat16
I32 = jnp.int32

LANES = 128
MXU_DEPTH = 256
HEAD_DIM = 64
NEG_INF = -1e30
EPS = 1e-6

A_HEADS = 8
A_BRANCHES = ((128, 1), (512, 4), (2048, 16))
B_HEADS = 8
B_KV_HEADS = 2
GRID_W = 64
ROPE_THETA = 10000.0
C_HEADS = 16
C_KV_HEADS = 4
C_RADIUS = 128
NUM_BUCKETS = 32
MAX_EXACT = 8
MAX_DISTANCE = 1024
N_EXPERTS = 16
CAPACITY_FACTOR = 2

MIB = 1024 * 1024
NT_DIMS = (((1,), (1,)), ((), ()))


def _params(semantics, vmem_mib):
    return pltpu.CompilerParams(dimension_semantics=semantics, vmem_limit_bytes=vmem_mib * MIB)


def _load_rows(x_ref, tm, nt, tiled):
    if not tiled:
        return x_ref[...]
    return jnp.concatenate([x_ref[pl.ds(s, tm, stride=nt), :] for s in range(nt)], axis=1)


def _store_rows_tiled(o_ref, val, tm, nt):
    for s in range(nt):
        o_ref[pl.ds(s, tm, stride=nt), :] = val[:, s * LANES:(s + 1) * LANES]


def _proj_body(x_ref, g_ref, w_ref, bd_ref, gains_ref, cos_ref, sin_ref, *out_refs, groups, tm, nt, tiled):
    x = _load_rows(x_ref, tm, nt, tiled)
    ms = jnp.mean(x * x, axis=-1, keepdims=True)
    xn = (x * lax.rsqrt(ms + EPS) * g_ref[...]).astype(BF16)
    lane = lax.broadcasted_iota(I32, (1, LANES), 1)
    first16 = (lane % 32) < 16
    bd = bd_ref[...]
    for (c0, width, kind, grow, _), o_ref in zip(groups, out_refs):
        y = jnp.dot(xn, w_ref[:, c0:c0 + width], preferred_element_type=F32)
        if kind == "plain":
            o_ref[...] = y.astype(o_ref.dtype)
            continue
        if kind == "value_ones":
            for j in range(width // LANES):
                o_ref[:, j * LANES:(j + 1) * LANES] = jnp.where(
                    lane < HEAD_DIM, y[:, j * LANES:(j + 1) * LANES], 1.0).astype(o_ref.dtype)
            continue
        gain = gains_ref[grow:grow + 1, :]
        for j2 in range(width // MXU_DEPTH):
            y2 = y[:, j2 * MXU_DEPTH:(j2 + 1) * MXU_DEPTH]
            ss2 = jnp.dot((y2 * y2).astype(BF16), bd, preferred_element_type=F32)
            for h in range(MXU_DEPTH // LANES):
                j = j2 * (MXU_DEPTH // LANES) + h
                yj = y2[:, h * LANES:(h + 1) * LANES]
                yn = yj * lax.rsqrt(ss2[:, h * LANES:(h + 1) * LANES] * (1.0 / HEAD_DIM) + EPS) * gain
                if kind == "rope":
                    partner = jnp.where(first16, pltpu.roll(yn, LANES - 16, 1), pltpu.roll(yn, 16, 1))
                    yn = yn * cos_ref[...] + partner * sin_ref[...]
                o_ref[:, j * LANES:(j + 1) * LANES] = yn.astype(o_ref.dtype)


def _norm_proj(x, g, w, bd, gains, cos_t, sin_t, groups, *, seq, tiled, tm=1024):
    d_model = w.shape[0]
    nt = d_model // LANES
    tokens = x.shape[0] // nt if tiled else x.shape[0]
    tm = min(tm, seq)
    nblk_seq = seq // tm
    x_spec = (pl.BlockSpec((tm * nt, LANES), lambda i: (i, 0)) if tiled
              else pl.BlockSpec((tm, d_model), lambda i: (i, 0)))
    const = lambda i: (0, 0)
    in_specs = [
        x_spec,
        pl.BlockSpec(g.shape, const),
        pl.BlockSpec(w.shape, const),
        pl.BlockSpec(bd.shape, const),
        pl.BlockSpec(gains.shape, const),
        pl.BlockSpec((tm, LANES), lambda i: (i % nblk_seq, 0)),
        pl.BlockSpec((tm, LANES), lambda i: (i % nblk_seq, 0)),
    ]
    out_shape = [jax.ShapeDtypeStruct((tokens, wd), dt) for (_, wd, _, _, dt) in groups]
    out_specs = [pl.BlockSpec((tm, wd), lambda i: (i, 0)) for (_, wd, _, _, _) in groups]
    return pl.pallas_call(
        functools.partial(_proj_body, groups=groups, tm=tm, nt=nt, tiled=tiled),
        grid=(tokens // tm,),
        in_specs=in_specs,
        out_specs=out_specs,
        out_shape=out_shape,
        compiler_params=_params(("parallel",), 48),
        name="norm_proj",
    )(x, g, w, bd, gains, cos_t, sin_t)


def _band_body(*refs, tq, qr, halo, length, has_state, final, has_sink):
    it = iter(refs)
    sink_ref = next(it) if has_sink else None
    q_ref, kp_ref, km_ref, kn_ref, vp_ref, vm_ref, vn_ref, bias_ref = (next(it) for _ in range(8))
    if has_state:
        m_in, l_in, a_in = next(it), next(it), next(it)
    if final:
        o_ref = next(it)
    else:
        m_out, l_out, a_out = next(it), next(it), next(it)

    cb = pl.program_id(1)
    blk = pl.program_id(2)
    win = qr + 2 * halo
    kext = jnp.concatenate([kp_ref[0], km_ref[0], kn_ref[0]], axis=0)
    vext = jnp.concatenate([vp_ref[0], vm_ref[0], vn_ref[0]], axis=0)
    lane = lax.broadcasted_iota(I32, (1, LANES), 1)
    lo = lane < HEAD_DIM
    col = lax.broadcasted_iota(I32, (1, win), 1)
    head_row = lax.broadcasted_iota(I32, (2 * qr, 1), 0) < qr
    bias = bias_ref[0]
    zero = jnp.zeros((), BF16)

    def split(tile):
        return jnp.concatenate([tile[:, 0:1], tile[:, HEAD_DIM:HEAD_DIM + 1]], axis=0)

    def merge(colv):
        return jnp.where(lo, colv[:qr], colv[qr:])

    for s in range(tq // qr):
        rows = slice(s * qr, (s + 1) * qr)
        qs = q_ref[0, rows, :]
        q2 = jnp.concatenate([jnp.where(lo, qs, zero), jnp.where(lo, zero, qs)], axis=0)
        kw = kext[s * qr:s * qr + win]
        vw = vext[s * qr:s * qr + win]
        sc = lax.dot_general(q2, kw, NT_DIMS, preferred_element_type=F32) + bias
        kpos = blk * tq + (s * qr - halo) + col
        sc = jnp.where((kpos >= 0) & (kpos < length), sc, NEG_INF)
        m_cur = jnp.max(sc, axis=1, keepdims=True)
        if has_state:
            m_prev = split(m_in[0, rows, :])
            l_prev = split(l_in[0, rows, :])
            m_new = jnp.maximum(m_prev, m_cur)
        else:
            m_new = m_cur
        if has_sink:
            sk = jnp.where(head_row, sink_ref[2 * cb], sink_ref[2 * cb + 1])
            m_new = jnp.maximum(m_new, sk)
        p = jnp.exp(sc - m_new)
        l_new = jnp.sum(p, axis=1, keepdims=True)
        pv = jnp.dot(p.astype(BF16), vw, preferred_element_type=F32)
        acc = merge(pv)
        if has_state:
            alpha = jnp.exp(m_prev - m_new)
            l_new = l_new + alpha * l_prev
            acc = acc + merge(alpha) * a_in[0, rows, :]
        if has_sink:
            l_new = l_new + jnp.exp(sk - m_new)
        if final:
            o_ref[0, rows, :] = (acc / merge(l_new)).astype(o_ref.dtype)
        else:
            m_out[0, rows, :] = jnp.broadcast_to(merge(m_new), (qr, LANES))
            l_out[0, rows, :] = jnp.broadcast_to(merge(l_new), (qr, LANES))
            a_out[0, rows, :] = acc


def _band_attention(q, k, v, bias, *, kv_div, bias_mod, qr, halo, state=None, final=True, sink=None, tq=512):
    bsz, length, qcols = q.shape
    ncb = qcols // LANES
    tq = min(tq, length)
    nq = length // tq
    per = tq // halo
    nh = length // halo
    win = qr + 2 * halo

    def qmap(b, cb, i, *_):
        return (b, i, cb)

    def kmain(b, cb, i, *_):
        return (b, i, cb // kv_div)

    def kprev(b, cb, i, *_):
        return (b, jnp.maximum(i * per - 1, 0), cb // kv_div)

    def knext(b, cb, i, *_):
        return (b, jnp.minimum((i + 1) * per, nh - 1), cb // kv_div)

    def bmap(b, cb, i, *_):
        return (cb % bias_mod, 0, 0)

    main = lambda m: pl.BlockSpec((1, tq, LANES), m)
    edge = lambda m: pl.BlockSpec((1, halo, LANES), m)
    in_specs = [main(qmap), edge(kprev), main(kmain), edge(knext), edge(kprev), main(kmain), edge(knext),
                pl.BlockSpec((1, 2 * qr, win), bmap)]
    args = [q, k, k, k, v, v, v, bias]
    if state is not None:
        in_specs += [main(qmap)] * 3
        args += list(state)
    if final:
        out_shape = jax.ShapeDtypeStruct(q.shape, BF16)
        out_specs = main(qmap)
    else:
        out_shape = [jax.ShapeDtypeStruct(q.shape, F32)] * 3
        out_specs = [main(qmap)] * 3
    has_sink = sink is not None
    grid_spec = pltpu.PrefetchScalarGridSpec(
        num_scalar_prefetch=1 if has_sink else 0,
        grid=(bsz, ncb, nq),
        in_specs=in_specs,
        out_specs=out_specs,
    )
    body = functools.partial(_band_body, tq=tq, qr=qr, halo=halo, length=length,
                             has_state=state is not None, final=final, has_sink=has_sink)
    call = pl.pallas_call(body, grid_spec=grid_spec, out_shape=out_shape,
                          compiler_params=_params(("parallel", "parallel", "parallel"), 48),
                          name="band_attention")
    return call(sink, *args) if has_sink else call(*args)


DIL_QR = 128


def _dilated_body(*refs, branches, tq, halo, seq, has_sink):
    sink_ref = refs[0] if has_sink else None
    q_ref, kp_ref, km_ref, kn_ref, vp_ref, vm_ref, vn_ref = refs[has_sink:has_sink + 7]
    rest = refs[has_sink + 7:]
    nb = len(branches)
    bias_refs = rest[:nb]
    o_ref = rest[nb]
    kext, vext, s_a, s_b, p_a, p_b = rest[nb + 1:nb + 7]
    stats = rest[nb + 7:]
    blk = pl.program_id(2)
    qr = DIL_QR
    kext[0:halo, :] = kp_ref[0]
    kext[halo:halo + tq, :] = km_ref[0]
    kext[halo + tq:halo + tq + halo, :] = kn_ref[0]
    vext[0:halo, :] = vp_ref[0]
    vext[halo:halo + tq, :] = vm_ref[0]
    vext[halo + tq:halo + tq + halo, :] = vn_ref[0]
    p_b[...] = jnp.zeros(p_b.shape, BF16)
    lane = lax.broadcasted_iota(I32, (1, LANES), 1)
    lo = lane < HEAD_DIM
    zero = jnp.zeros((), BF16)
    nunits = tq // qr

    for bi, (dil, radius) in enumerate(branches):
        win = qr + 2 * radius
        m_st, l_st, a_st = stats[3 * bi:3 * bi + 3]
        bias_ref = bias_refs[bi]
        upr = nunits // dil
        off = halo - radius * dil
        col = lax.broadcasted_iota(I32, (1, win), 1)

        def first_row(u, dil=dil, upr=upr):
            u = jnp.clip(u, 0, nunits - 1)
            start = u // upr + (u % upr) * (qr * dil)
            return pl.multiple_of(start, qr) if dil == 1 else start

        def rows(start, n, dil=dil):
            return pl.ds(start, n) if dil == 1 else pl.ds(start, n, stride=dil)

        def scores(u, s_ref, first_row=first_row, rows=rows, off=off, win=win):
            start = first_row(u)
            qs = q_ref[0, rows(start, qr), :].astype(BF16)
            q2 = jnp.concatenate([jnp.where(lo, qs, zero), jnp.where(lo, zero, qs)], axis=0)
            kw = kext[rows(off + start, win), :].astype(BF16)
            s_ref[:, 0:win] = lax.dot_general(q2, kw, NT_DIMS, preferred_element_type=F32)

        def softmax(u, s_ref, p_ref, first_row=first_row, rows=rows, win=win, dil=dil, radius=radius,
                    bias_ref=bias_ref, m_st=m_st, l_st=l_st, col=col):
            start = first_row(u)
            sc = s_ref[:, 0:win] + bias_ref[0]
            kpos = blk * tq + start - radius * dil + dil * col
            sc = jnp.where((kpos >= 0) & (kpos < seq), sc, NEG_INF)
            m = jnp.max(sc, axis=1, keepdims=True)
            p = jnp.exp2(sc - m)
            l = jnp.sum(p, axis=1, keepdims=True)
            p_ref[:, 0:win] = p.astype(BF16)
            m_st[rows(start, qr), :] = jnp.where(lo, m[:qr], m[qr:])
            l_st[rows(start, qr), :] = jnp.where(lo, l[:qr], l[qr:])

        def values(u, p_ref, first_row=first_row, rows=rows, off=off, win=win, a_st=a_st):
            start = first_row(u)
            vw = vext[rows(off + start, win), :].astype(BF16)
            pv = jnp.dot(p_ref[:, 0:win], vw, preferred_element_type=F32)
            a_st[rows(start, qr), :] = jnp.where(lo, pv[:qr], pv[qr:])

        scores(0, s_a)

        def pair(i, carry, scores=scores, softmax=softmax, values=values):
            u = 2 * i
            scores(u + 1, s_b)
            values(u - 1, p_b)
            softmax(u, s_a, p_a)
            scores(u + 2, s_a)
            values(u, p_a)
            softmax(u + 1, s_b, p_b)
            return carry

        lax.fori_loop(0, nunits // 2, pair, 0)
        values(nunits - 1, p_b)

    if has_sink:
        cb = pl.program_id(1)
        sink_row = jnp.where(lo, sink_ref[2 * cb], sink_ref[2 * cb + 1])
    chunk = 256
    for c in range(tq // chunk):
        rws = slice(c * chunk, (c + 1) * chunk)
        ms = [stats[3 * bi][rws, :] for bi in range(nb)]
        mx = functools.reduce(jnp.maximum, ms)
        num = None
        den = None
        if has_sink:
            mx = jnp.maximum(mx, sink_row)
            den = jnp.exp2(sink_row - mx)
        for bi in range(nb):
            w = jnp.exp2(ms[bi] - mx)
            n_b = w * stats[3 * bi + 2][rws, :]
            d_b = w * stats[3 * bi + 1][rws, :]
            num = n_b if num is None else num + n_b
            den = d_b if den is None else den + d_b
        o_ref[0, rws, :] = (num / den).astype(o_ref.dtype)


def _dilated_attention(q, k, v, biases, branches, *, kv_div=1, sink=None, tq=2048, name="dilated_attention"):
    bsz, seq, cols = q.shape
    ncb = cols // LANES
    halo = max(d * r for d, r in branches)
    tq = min(tq, seq)
    assert tq % halo == 0 and seq % tq == 0 and (tq // DIL_QR) % 2 == 0
    assert all((tq // DIL_QR) % d == 0 for d, _ in branches)
    per = tq // halo
    nh = seq // halo
    maxwin = max(DIL_QR + 2 * r for _, r in branches)

    qmain = pl.BlockSpec((1, tq, LANES), lambda b, cb, i, *_: (b, i, cb))
    main = pl.BlockSpec((1, tq, LANES), lambda b, cb, i, *_: (b, i, cb // kv_div))
    prev = pl.BlockSpec((1, halo, LANES), lambda b, cb, i, *_: (b, jnp.maximum(i * per - 1, 0), cb // kv_div))
    nxt = pl.BlockSpec((1, halo, LANES), lambda b, cb, i, *_: (b, jnp.minimum((i + 1) * per, nh - 1), cb // kv_div))
    bias_specs = [pl.BlockSpec((1,) + bz.shape[1:], lambda b, cb, i, *_: (cb, 0, 0)) for bz in biases]
    ext = tq + 2 * halo
    scratch = ([pltpu.VMEM((ext, LANES), k.dtype)] * 2
               + [pltpu.VMEM((2 * DIL_QR, maxwin), F32)] * 2
               + [pltpu.VMEM((2 * DIL_QR, maxwin), BF16)] * 2
               + [pltpu.VMEM((tq, LANES), F32)] * (3 * len(branches)))
    has_sink = sink is not None
    grid_spec = pltpu.PrefetchScalarGridSpec(
        num_scalar_prefetch=int(has_sink),
        grid=(bsz, ncb, seq // tq),
        in_specs=[qmain, prev, main, nxt, prev, main, nxt] + bias_specs,
        out_specs=qmain,
        scratch_shapes=scratch,
    )
    call = pl.pallas_call(
        functools.partial(_dilated_body, branches=tuple(branches), tq=tq, halo=halo, seq=seq, has_sink=has_sink),
        grid_spec=grid_spec,
        out_shape=jax.ShapeDtypeStruct(q.shape, BF16),
        compiler_params=_params(("parallel", "parallel", "parallel"), 48),
        name=name,
    )
    args = (q, k, k, k, v, v, v, *biases)
    return call(sink, *args) if has_sink else call(*args)


def _dense_body(q_ref, k_ref, v_ref, o_ref, q4_ref, s_a, s_b, p_a, p_b, al_a, al_b, m_ref, acc_ref, *, tq, tk, nk, rb):
    lane = lax.broadcasted_iota(I32, (1, LANES), 1)
    lo = lane < HEAD_DIM
    zero = jnp.zeros((), BF16)
    for h in range(4):
        blk = q_ref[0, :, (h // 2) * LANES:(h // 2 + 1) * LANES]
        q4_ref[h * tq:(h + 1) * tq, :] = jnp.where(lo, blk, zero) if h % 2 == 0 else jnp.where(lo, zero, blk)
    m_ref[...] = jnp.full(m_ref.shape, NEG_INF, F32)
    acc_ref[...] = jnp.zeros(acc_ref.shape, F32)
    p_b[...] = jnp.zeros(p_b.shape, BF16)
    al_b[...] = jnp.ones(al_b.shape, F32)
    nct = tk // LANES

    def scores(j, s_ref):
        start = pl.multiple_of(jnp.minimum(j, nk - 1) * tk, tk)
        s_ref[...] = lax.dot_general(q4_ref[...], k_ref[0, pl.ds(start, tk), :], NT_DIMS, preferred_element_type=F32)

    def weighted_values(j, p_ref, al_ref):
        start = pl.multiple_of(jnp.maximum(j, 0) * tk, tk)
        pv = None
        for c in range(tk // MXU_DEPTH):
            part = jnp.dot(p_ref[:, c * MXU_DEPTH:(c + 1) * MXU_DEPTH],
                           v_ref[0, pl.ds(pl.multiple_of(start + c * MXU_DEPTH, MXU_DEPTH), MXU_DEPTH), :],
                           preferred_element_type=F32)
            pv = part if pv is None else pv + part
        acc_ref[...] = al_ref[...] * acc_ref[...] + pv

    def softmax_update(s_ref, p_ref, al_ref):
        for r in range(4 * tq // rb):
            rows = slice(r * rb, (r + 1) * rb)
            mt = functools.reduce(jnp.maximum, [s_ref[rows, c * LANES:(c + 1) * LANES] for c in range(nct)])
            m_prev = m_ref[rows, :]
            m_new = jnp.maximum(m_prev, jnp.max(mt, axis=1, keepdims=True))
            al_ref[rows, :] = jnp.exp2(m_prev - m_new)
            m_ref[rows, :] = m_new
        for r in range(4 * tq // rb):
            rows = slice(r * rb, (r + 1) * rb)
            m_new = m_ref[rows, :]
            for c in range(nct):
                cols = slice(c * LANES, (c + 1) * LANES)
                p_ref[rows, cols] = jnp.exp2(s_ref[rows, cols] - m_new).astype(BF16)

    scores(0, s_a)

    def pair(i, carry):
        j = 2 * i
        scores(j + 1, s_b)
        weighted_values(j - 1, p_b, al_b)
        softmax_update(s_a, p_a, al_a)
        scores(j + 2, s_a)
        weighted_values(j, p_a, al_a)
        softmax_update(s_b, p_b, al_b)
        return carry

    lax.fori_loop(0, nk // 2, pair, 0)
    weighted_values(nk - 1, p_b, al_b)
    acc = acc_ref[...]
    o = acc / jnp.where(lo, pltpu.roll(acc, HEAD_DIM, 1), acc)
    for hp in range(2):
        even = o[(2 * hp) * tq:(2 * hp + 1) * tq]
        odd = pltpu.roll(o[(2 * hp + 1) * tq:(2 * hp + 2) * tq], HEAD_DIM, 1)
        o_ref[0, :, hp * LANES:(hp + 1) * LANES] = jnp.where(lo, even, odd).astype(o_ref.dtype)


def _dense_attention(q, k, v, *, tq=128, tk=512, rb=32):
    bsz, seq, _ = q.shape
    tq = min(tq, seq)
    tk = min(tk, seq // 2)
    assert seq % (2 * tk) == 0 and seq % tq == 0
    ngroups = k.shape[2] // LANES
    return pl.pallas_call(
        functools.partial(_dense_body, tq=tq, tk=tk, nk=seq // tk, rb=min(rb, 4 * tq)),
        grid=(bsz, ngroups, seq // tq),
        in_specs=[
            pl.BlockSpec((1, tq, 2 * LANES), lambda b, g, i: (b, i, g)),
            pl.BlockSpec((1, seq, LANES), lambda b, g, i: (b, 0, g)),
            pl.BlockSpec((1, seq, LANES), lambda b, g, i: (b, 0, g)),
        ],
        out_specs=pl.BlockSpec((1, tq, 2 * LANES), lambda b, g, i: (b, i, g)),
        out_shape=jax.ShapeDtypeStruct(q.shape, BF16),
        scratch_shapes=(
            [pltpu.VMEM((4 * tq, LANES), BF16)]
            + [pltpu.VMEM((4 * tq, tk), F32)] * 2
            + [pltpu.VMEM((4 * tq, tk), BF16)] * 2
            + [pltpu.VMEM((4 * tq, LANES), F32)] * 4
        ),
        compiler_params=_params(("parallel", "parallel", "parallel"), 48),
        name="dense_attention",
    )(q, k, v)


def _outproj_body(*refs, n_in, tm, nt, tiled):
    o_refs = refs[:n_in]
    w_ref, x_ref, g_ref, wr_ref, x2_ref, hn_ref, aff_ref = refs[n_in:]
    y = None
    c0 = 0
    for o_ref in o_refs:
        wd = o_ref.shape[1]
        part = jnp.dot(o_ref[...], w_ref[c0:c0 + wd, :], preferred_element_type=F32)
        y = part if y is None else y + part
        c0 += wd
    x2 = _load_rows(x_ref, tm, nt, tiled) + y
    ms = jnp.mean(x2 * x2, axis=-1, keepdims=True)
    hn = x2 * lax.rsqrt(ms + EPS) * g_ref[...]
    _store_rows_tiled(x2_ref, x2, tm, nt)
    _store_rows_tiled(hn_ref, hn, tm, nt)
    wr = wr_ref[...]
    wr_hi = wr.astype(BF16)
    wr_lo = (wr - wr_hi.astype(F32)).astype(BF16)
    hn_hi = hn.astype(BF16)
    hn_lo = (hn - hn_hi.astype(F32)).astype(BF16)
    lt = (lax.dot_general(wr_hi, hn_hi, NT_DIMS, preferred_element_type=F32)
          + lax.dot_general(wr_lo, hn_hi, NT_DIMS, preferred_element_type=F32)
          + lax.dot_general(wr_hi, hn_lo, NT_DIMS, preferred_element_type=F32))
    ex = jnp.exp(lt - jnp.max(lt, axis=0, keepdims=True))
    aff_ref[0] = ex / jnp.sum(ex, axis=0, keepdims=True)


def _out_proj(o_parts, w, x, g, wr_t, *, bsz, seq, tiled, tm=1024):
    d_model = w.shape[1]
    nt = d_model // LANES
    tokens = bsz * seq
    tm = min(tm, seq)
    nblk_seq = seq // tm
    nexp = wr_t.shape[0]
    const = lambda i: (0, 0)
    x_spec = (pl.BlockSpec((tm * nt, LANES), lambda i: (i, 0)) if tiled
              else pl.BlockSpec((tm, d_model), lambda i: (i, 0)))
    in_specs = [pl.BlockSpec((tm, o.shape[1]), lambda i: (i, 0)) for o in o_parts]
    in_specs += [pl.BlockSpec(w.shape, const), x_spec, pl.BlockSpec(g.shape, const), pl.BlockSpec(wr_t.shape, const)]
    tiled_spec = pl.BlockSpec((tm * nt, LANES), lambda i: (i, 0))
    return pl.pallas_call(
        functools.partial(_outproj_body, n_in=len(o_parts), tm=tm, nt=nt, tiled=tiled),
        grid=(tokens // tm,),
        in_specs=in_specs,
        out_specs=[tiled_spec, tiled_spec,
                   pl.BlockSpec((1, nexp, tm), lambda i: (i // nblk_seq, 0, i % nblk_seq))],
        out_shape=[jax.ShapeDtypeStruct((tokens * nt, LANES), F32),
                   jax.ShapeDtypeStruct((tokens * nt, LANES), F32),
                   jax.ShapeDtypeStruct((bsz, nexp, seq), F32)],
        compiler_params=_params(("parallel",), 48),
        name="out_proj_router",
    )(*o_parts, w, x, g, wr_t)


INDEX_CHUNK = 128


def _block_cumsum(x, tri, strict_lower):
    within = jnp.dot(x.astype(BF16), tri, preferred_element_type=F32)
    total = jnp.broadcast_to(within[:, LANES - 1:LANES], within.shape)
    before = jnp.dot(strict_lower, total.astype(BF16), preferred_element_type=F32)
    return within + before, before + total


def _route_body(aff_ref, aff4_ref, idx_ref, thr_ref, need_ref, *, seq, cap, nexp):
    e = pl.program_id(1)
    nblk = seq // LANES

    @pl.when(e == 0)
    def _():
        bits = pltpu.bitcast(aff_ref[0], I32)

        def search(_, c):
            lo, hi = c
            mid = lo + ((hi - lo + 1) >> 1)
            cnt = jnp.sum(jnp.where(bits >= mid, 1.0, 0.0), axis=1, keepdims=True)
            ok = cnt >= cap
            return jnp.where(ok, mid, lo), jnp.where(ok, hi, mid - 1)

        lo0 = jnp.zeros((nexp, 1), I32)
        hi0 = jnp.full((nexp, 1), 0x7F800000, I32)
        thr, _ = lax.fori_loop(0, 32, search, (lo0, hi0))
        need = cap - jnp.sum(jnp.where(bits > thr, 1.0, 0.0), axis=1, keepdims=True)
        thr_ref[...] = jnp.broadcast_to(thr, (nexp, LANES))
        need_ref[...] = jnp.broadcast_to(need, (nexp, LANES))

    bits = pltpu.bitcast(aff4_ref[0, 0], I32)
    thr = thr_ref[pl.ds(e, 1), :]
    need = need_ref[pl.ds(e, 1), :]
    lane = lax.broadcasted_iota(I32, (1, LANES), 1)
    r = lax.broadcasted_iota(I32, (LANES, LANES), 0)
    c = lax.broadcasted_iota(I32, (LANES, LANES), 1)
    tri = jnp.where(r <= c, 1.0, 0.0).astype(BF16)
    eye = r == c
    strict_lower = jnp.where(lax.broadcasted_iota(I32, (nblk, nblk), 1) < lax.broadcasted_iota(I32, (nblk, nblk), 0),
                             1.0, 0.0).astype(BF16)
    eye_rows = lax.broadcasted_iota(I32, (nblk, LANES), 0) == lax.broadcasted_iota(I32, (nblk, LANES), 1)
    gt = bits > thr
    eq = bits == thr
    eq_rank, _ = _block_cumsum(jnp.where(eq, 1.0, 0.0), tri, strict_lower)
    sel = gt | (eq & (eq_rank <= need))
    cs, row_end = _block_cumsum(jnp.where(sel, 1.0, 0.0), tri, strict_lower)
    end_lanes = jnp.sum(jnp.where(eye_rows, row_end, 0.0), axis=0, keepdims=True)
    end_lanes = jnp.where(lane < nblk, end_lanes, 2.0 * seq)
    hi = jnp.floor(cs * (1.0 / LANES))
    lo = cs - LANES * hi
    pad = jnp.zeros((LANES - nblk, LANES), F32)
    hi = jnp.concatenate([hi, pad], axis=0).astype(BF16)
    lo = jnp.concatenate([lo, pad], axis=0).astype(BF16)
    lane_f = lane.astype(F32)
    for cc in range(cap // INDEX_CHUNK):
        slot = (lax.broadcasted_iota(I32, (INDEX_CHUNK, 1), 0) + cc * INDEX_CHUNK).astype(F32)
        krow = jnp.sum(jnp.where(end_lanes <= slot, 1.0, 0.0), axis=1, keepdims=True)
        onehot = jnp.where(lane_f == krow, 1.0, 0.0).astype(BF16)
        counts = (LANES * jnp.dot(onehot, hi, preferred_element_type=F32)
                  + jnp.dot(onehot, lo, preferred_element_type=F32))
        inrow = jnp.sum(jnp.where(counts <= slot, 1.0, 0.0), axis=1, keepdims=True)
        token = LANES * krow + inrow
        idx_ref[0, :, cc * INDEX_CHUNK:(cc + 1) * INDEX_CHUNK] = jnp.sum(
            jnp.where(eye, token, 0.0), axis=0, keepdims=True).astype(I32)


def _route(aff_t, cap):
    bsz, nexp, seq = aff_t.shape
    nblk = seq // LANES
    assert nblk <= LANES and cap % INDEX_CHUNK == 0
    return pl.pallas_call(
        functools.partial(_route_body, seq=seq, cap=cap, nexp=nexp),
        grid=(bsz, nexp),
        in_specs=[pl.BlockSpec((1, nexp, seq), lambda b, e: (b, 0, 0)),
                  pl.BlockSpec((1, 1, nblk, LANES), lambda b, e: (b, e, 0, 0))],
        out_specs=pl.BlockSpec((1, 1, cap), lambda b, e: (b * nexp + e, 0, 0)),
        out_shape=jax.ShapeDtypeStruct((bsz * nexp, 1, cap), I32),
        scratch_shapes=[pltpu.VMEM((nexp, LANES), I32), pltpu.VMEM((nexp, LANES), F32)],
        compiler_params=_params(("parallel", "arbitrary"), 48),
        name="route",
    )(aff_t, aff_t.reshape(bsz, nexp, nblk, LANES))


ROW_UNROLL = 8


def _gather_body(idx_ref, h_hbm, x_ref, hbuf, xbuf, sem, *, cap, nt):
    b = pl.program_id(0)
    e = pl.program_id(1)

    @pl.when(e == 0)
    def _():
        cp = pltpu.make_async_copy(h_hbm.at[b], hbuf, sem)
        cp.start()
        cp.wait()

    def rows(i, carry):
        for u in range(ROW_UNROLL):
            c = i * ROW_UNROLL + u
            t = idx_ref[0, 0, c]
            xbuf[pl.ds(pl.multiple_of(c * nt, nt), nt), :] = hbuf[pl.ds(pl.multiple_of(t * nt, nt), nt), :]
        return carry

    lax.fori_loop(0, cap // ROW_UNROLL, rows, 0)
    for s in range(nt):
        x_ref[0, :, s * LANES:(s + 1) * LANES] = xbuf[pl.ds(s, cap, stride=nt), :].astype(x_ref.dtype)


def _gather(idx, h_tiled, *, bsz, nexp, seq, cap, nt):
    return pl.pallas_call(
        functools.partial(_gather_body, cap=cap, nt=nt),
        grid=(bsz, nexp),
        in_specs=[pl.BlockSpec((1, 1, cap), lambda b, e: (b * nexp + e, 0, 0), memory_space=pltpu.SMEM),
                  pl.BlockSpec(memory_space=pl.ANY)],
        out_specs=pl.BlockSpec((1, cap, nt * LANES), lambda b, e: (e, b, 0)),
        out_shape=jax.ShapeDtypeStruct((nexp, bsz * cap, nt * LANES), BF16),
        scratch_shapes=[pltpu.VMEM((seq * nt, LANES), F32), pltpu.VMEM((cap * nt, LANES), F32),
                        pltpu.SemaphoreType.DMA(())],
        compiler_params=_params(("arbitrary", "arbitrary"), 56),
        name="moe_gather",
    )(idx, h_tiled)


def _ffn_body(x_ref, wg_ref, wu_ref, wd_ref, y_ref, acc_ref, *, cap, nt, mb):
    f = pl.program_id(2)
    x = x_ref[0]
    g = jnp.dot(x, wg_ref[0].astype(BF16), preferred_element_type=F32)
    u = jnp.dot(x, wu_ref[0].astype(BF16), preferred_element_type=F32)
    act = (g * (1.0 / (1.0 + jnp.exp(-g))) * u).astype(BF16)
    y = jnp.dot(act, wd_ref[0].astype(BF16), preferred_element_type=F32)

    @pl.when(f == 0)
    def _():
        acc_ref[...] = y

    @pl.when(f > 0)
    def _():
        acc_ref[...] += y

    @pl.when(f == pl.num_programs(2) - 1)
    def _():
        for bb in range(mb):
            _store_rows_tiled(y_ref.at[0, bb], acc_ref[bb * cap:(bb + 1) * cap, :], cap, nt)


def _ffn(x, wg, wu, wd, *, bsz, cap, tf=512, mb=1):
    nexp, d_model, d_exp = wg.shape
    nt = d_model // LANES
    tf = min(tf, d_exp)
    mb = mb if bsz % mb == 0 else 1
    return pl.pallas_call(
        functools.partial(_ffn_body, cap=cap, nt=nt, mb=mb),
        grid=(nexp, bsz // mb, d_exp // tf),
        in_specs=[pl.BlockSpec((1, mb * cap, d_model), lambda e, b, f: (e, b, 0)),
                  pl.BlockSpec((1, d_model, tf), lambda e, b, f: (e, 0, f)),
                  pl.BlockSpec((1, d_model, tf), lambda e, b, f: (e, 0, f)),
                  pl.BlockSpec((1, tf, d_model), lambda e, b, f: (e, f, 0))],
        out_specs=pl.BlockSpec((1, mb, cap * nt, LANES), lambda e, b, f: (e, b, 0, 0)),
        out_shape=jax.ShapeDtypeStruct((nexp, bsz, cap * nt, LANES), F32),
        scratch_shapes=[pltpu.VMEM((mb * cap, d_model), F32)],
        compiler_params=_params(("parallel", "parallel", "arbitrary"), 60),
        name="moe_ffn",
    )(x, wg, wu, wd)


def _combine_body(idx_ref, aff_ref, x_hbm, y_ref, o_hbm, acc, sem, *, cap, nt, nexp):
    b = pl.program_id(0)
    e = pl.program_id(1)

    @pl.when(e == 0)
    def _():
        cp = pltpu.make_async_copy(x_hbm.at[b], acc, sem)
        cp.start()
        cp.wait()

    def rows(i, carry):
        dst, vals = [], []
        for u in range(ROW_UNROLL):
            c = i * ROW_UNROLL + u
            t = idx_ref[0, 0, c]
            gate = aff_ref[0, 0, t]
            d = pl.ds(pl.multiple_of(t * nt, nt), nt)
            dst.append(d)
            vals.append(acc[d, :] + gate * y_ref[0, 0, pl.ds(pl.multiple_of(c * nt, nt), nt), :])
        for d, val in zip(dst, vals):
            acc[d, :] = val
        return carry

    lax.fori_loop(0, cap // ROW_UNROLL, rows, 0)

    @pl.when(e == nexp - 1)
    def _():
        cp = pltpu.make_async_copy(acc, o_hbm.at[b], sem)
        cp.start()
        cp.wait()


def _combine(idx, aff_rows, x_tiled, y_tiled, *, bsz, nexp, seq, cap, nt):
    return pl.pallas_call(
        functools.partial(_combine_body, cap=cap, nt=nt, nexp=nexp),
        grid=(bsz, nexp),
        in_specs=[pl.BlockSpec((1, 1, cap), lambda b, e: (b * nexp + e, 0, 0), memory_space=pltpu.SMEM),
                  pl.BlockSpec((1, 1, seq), lambda b, e: (b * nexp + e, 0, 0), memory_space=pltpu.SMEM),
                  pl.BlockSpec(memory_space=pl.ANY),
                  pl.BlockSpec((1, 1, cap * nt, LANES), lambda b, e: (e, b, 0, 0))],
        out_specs=pl.BlockSpec(memory_space=pl.ANY),
        out_shape=jax.ShapeDtypeStruct(x_tiled.shape, F32),
        scratch_shapes=[pltpu.VMEM((seq * nt, LANES), F32), pltpu.SemaphoreType.DMA(())],
        compiler_params=_params(("arbitrary", "arbitrary"), 56),
        name="moe_combine",
    )(idx, aff_rows, x_tiled, y_tiled)


def _moe(x2_t, hn_t, aff_t, wg, wu, wd, *, bsz, seq):
    nexp = aff_t.shape[1]
    nt = wg.shape[1] // LANES
    cap = CAPACITY_FACTOR * seq // nexp
    idx = _route(aff_t, cap)
    xg = _gather(idx, hn_t.reshape(bsz, seq * nt, LANES), bsz=bsz, nexp=nexp, seq=seq, cap=cap, nt=nt)
    y = _ffn(xg, wg, wu, wd, bsz=bsz, cap=cap)
    out = _combine(idx, aff_t.reshape(bsz * nexp, 1, seq), x2_t.reshape(bsz, seq * nt, LANES), y,
                   bsz=bsz, nexp=nexp, seq=seq, cap=cap, nt=nt)
    return out.reshape(bsz * seq * nt, LANES)


def _t5_bucket(rel):
    half = NUM_BUCKETS // 2
    ret = jnp.where(rel > 0, half, 0)
    n = jnp.abs(rel)
    nf = jnp.maximum(n, 1).astype(F32)
    large = MAX_EXACT + (jnp.log(nf / MAX_EXACT) / math.log(MAX_DISTANCE / MAX_EXACT)
                         * (half - MAX_EXACT)).astype(I32)
    large = jnp.minimum(large, half - 1)
    return ret + jnp.where(n < MAX_EXACT, n, large)


def _band_bias(rel_bias, nheads, qr, halo, radius, dil):
    rel = jnp.arange(qr + 2 * halo)[None, :] - halo - jnp.arange(qr)[:, None]
    onehot = (_t5_bucket(rel * dil)[None] == jnp.arange(NUM_BUCKETS)[:, None, None]).astype(F32)
    tile = jnp.sum(rel_bias[:nheads].astype(F32)[:, :, None, None] * onehot[None], axis=1)
    tile = jnp.where((jnp.abs(rel) <= radius)[None], tile, NEG_INF)
    return tile.reshape(nheads // 2, 2 * qr, qr + 2 * halo)


def _tile_gain(g, scale=1.0):
    return jnp.tile(g.astype(F32) * scale, LANES // HEAD_DIM)[None, :]


def _dup_heads(w, nheads):
    d = w.shape[0]
    w4 = w.reshape(d, nheads, 1, HEAD_DIM)
    return jnp.broadcast_to(w4, (d, nheads, 2, HEAD_DIM)).reshape(d, nheads * 2 * HEAD_DIM)


def _rope_tables(seq):
    half = HEAD_DIM // 2
    inv = ROPE_THETA ** (-jnp.arange(0, half, 2, dtype=F32) / half)
    t = jnp.arange(seq)
    row = (t // GRID_W).astype(F32)[:, None] * inv
    colp = (t % GRID_W).astype(F32)[:, None] * inv
    cos64 = jnp.concatenate([jnp.cos(row), jnp.cos(row), jnp.cos(colp), jnp.cos(colp)], axis=1)
    sin64 = jnp.concatenate([-jnp.sin(row), jnp.sin(row), -jnp.sin(colp), jnp.sin(colp)], axis=1)
    return jnp.tile(cos64, (1, 2)), jnp.tile(sin64, (1, 2))


def _block_diag_ones():
    r = jnp.arange(MXU_DEPTH)
    return (r[:, None] // HEAD_DIM == r[None, :] // HEAD_DIM).astype(BF16)


def kernel(x, rel_bias, l0_norm_attn, l0_w_in, l0_a_qnorm, l0_a_knorm, l0_b_qnorm, l0_b_knorm, l0_w_out,
           l0_norm_ffn, l0_router, l0_w_gate, l0_w_up, l0_w_down, l1_norm_attn, l1_w_in, l1_c_qnorm, l1_c_knorm,
           l1_sink, l1_w_out, l1_norm_ffn, l1_router, l1_w_gate, l1_w_up, l1_w_down):
    bsz, seq, d_model = x.shape
    nt = d_model // LANES
    scale = 1.0 / math.sqrt(HEAD_DIM)
    a_w = A_HEADS * HEAD_DIM
    bq_w = B_HEADS * HEAD_DIM
    bkv_w = B_KV_HEADS * HEAD_DIM
    cq_w = C_HEADS * HEAD_DIM
    ckv_w = C_KV_HEADS * HEAD_DIM
    bd = _block_diag_ones()
    cos_t, sin_t = _rope_tables(seq)

    c = 3 * a_w + bq_w
    w0 = jnp.concatenate([l0_w_in[:, :c], _dup_heads(l0_w_in[:, c:c + bkv_w], B_KV_HEADS),
                          _dup_heads(l0_w_in[:, c + bkv_w:], B_KV_HEADS)], axis=1).astype(BF16)
    log2e = math.log2(math.e)
    gains0 = jnp.concatenate([_tile_gain(l0_a_qnorm, scale * log2e), _tile_gain(l0_a_knorm),
                              _tile_gain(l0_b_qnorm, scale * log2e), _tile_gain(l0_b_knorm)], axis=0)
    groups0 = ((0, a_w, "norm", 0, F32), (a_w, a_w, "norm", 1, F32), (2 * a_w, a_w, "plain", 0, F32),
               (3 * a_w, bq_w, "rope", 2, BF16), (c, 2 * bkv_w, "rope", 3, BF16),
               (c + 2 * bkv_w, 2 * bkv_w, "value_ones", 0, BF16))
    qa, ka, va, qb, kb, vb = _norm_proj(x.reshape(bsz * seq, d_model), l0_norm_attn[None, :], w0, bd, gains0,
                                        cos_t, sin_t, groups0, seq=seq, tiled=False)
    branches = tuple((dil, window // (2 * dil)) for window, dil in A_BRANCHES)
    biases = [_band_bias(rel_bias, A_HEADS, DIL_QR, radius, radius, dil) * log2e for dil, radius in branches]
    shp = lambda t: t.reshape(bsz, seq, t.shape[-1])
    oa = _dilated_attention(shp(qa), shp(ka), shp(va), biases, branches).reshape(bsz * seq, a_w)
    ob = _dense_attention(qb.reshape(bsz, seq, bq_w), kb.reshape(bsz, seq, 2 * bkv_w),
                          vb.reshape(bsz, seq, 2 * bkv_w)).reshape(bsz * seq, bq_w)
    x2, hn, aff = _out_proj([oa, ob], l0_w_out.astype(BF16), x.reshape(bsz * seq, d_model), l0_norm_ffn[None, :],
                            l0_router.T, bsz=bsz, seq=seq, tiled=False)
    x3 = _moe(x2, hn, aff, l0_w_gate, l0_w_up, l0_w_down, bsz=bsz, seq=seq)

    w1 = jnp.concatenate([l1_w_in[:, :cq_w], _dup_heads(l1_w_in[:, cq_w:cq_w + ckv_w], C_KV_HEADS),
                          _dup_heads(l1_w_in[:, cq_w + ckv_w:], C_KV_HEADS)], axis=1).astype(BF16)
    gains1 = jnp.concatenate([_tile_gain(l1_c_qnorm, scale * log2e), _tile_gain(l1_c_knorm)], axis=0)
    groups1 = ((0, cq_w, "norm", 0, BF16), (cq_w, 2 * ckv_w, "norm", 1, BF16),
               (cq_w + 2 * ckv_w, 2 * ckv_w, "plain", 0, BF16))
    qc, kc, vc = _norm_proj(x3, l1_norm_attn[None, :], w1, bd, gains1, cos_t, sin_t, groups1, seq=seq, tiled=True)
    bias_c = _band_bias(rel_bias, C_HEADS, DIL_QR, C_RADIUS, C_RADIUS, 1) * log2e
    oc = _dilated_attention(shp(qc), shp(kc), shp(vc), [bias_c], ((1, C_RADIUS),), kv_div=C_HEADS // C_KV_HEADS // 2,
                            sink=l1_sink.astype(F32) * log2e, name="window_attention").reshape(bsz * seq, cq_w)
    x4, hn1, aff1 = _out_proj([oc], l1_w_out.astype(BF16), x3, l1_norm_ffn[None, :], l1_router.T,
                              bsz=bsz, seq=seq, tiled=True)
    x5 = _moe(x4, hn1, aff1, l1_w_gate, l1_w_up, l1_w_down, bsz=bsz, seq=seq)
    return x5.reshape(bsz, seq, nt, LANES).reshape(bsz, seq, d_model)
```
